```python
import jax, jax.numpy as jnp
from jax import lax
import numpy as np

D_MODEL = 1024
BATCH = 8
SEQ = 8192
DEPTH = 2

GDN_HEADS = 4
GDN_DK = 128
GDN_DV = 128
GDN_CHUNK = 64
CONV_WIDTH = 5
ATT_GROUPS = ((128, 1), (512, 4), (2048, 16))
ATT_HEADS_PER_GROUP = 4
ATT_HEADS = ATT_HEADS_PER_GROUP * len(ATT_GROUPS)
ATT_HEAD_DIM = 64
ROPE_THETA = 500000.0
ROPE_DIMS = ATT_HEAD_DIM // 4
N_EXPERTS = 16
EXPERT_FF = 1024
EC_CAPACITY_FACTOR = 2
RMS_EPS = 1e-6
NEG_BIG = -1e30

GDN_QK_W = GDN_HEADS * GDN_DK
GDN_V_W = GDN_HEADS * GDN_DV
GDN_CONV_CH = 2 * GDN_QK_W + GDN_V_W
ATT_W = ATT_HEADS * ATT_HEAD_DIM
ATT_OUT_W = ATT_HEADS_PER_GROUP * ATT_HEAD_DIM
N_BRANCHES = 2
OFF_Z = GDN_CONV_CH
OFF_BETA = OFF_Z + GDN_V_W
OFF_DECAY = OFF_BETA + 2 * GDN_HEADS
OFF_ATT = OFF_DECAY + 2 * GDN_HEADS
OFF_GATE = OFF_ATT + 3 * ATT_W
P_IN = OFF_GATE + N_BRANCHES * D_MODEL

kernel_name = "hybrid_gdn_dilated_attn_ec_moe"


def rms_norm(x, w):
    xf = x.astype(jnp.float32)
    y = xf * lax.rsqrt(jnp.mean(xf * xf, axis=-1, keepdims=True) + RMS_EPS)
    return (y * w.astype(jnp.float32)).astype(x.dtype)


def l2_normalize(x):
    return x * lax.rsqrt(jnp.sum(x * x, axis=-1, keepdims=True) + 1e-6)


def short_conv(u, w):
    c = u.shape[-1]
    pad = (CONV_WIDTH - 1) // 2
    return lax.conv_general_dilated(u, w[:, None, :].astype(u.dtype), window_strides=(1,),
                                    padding=((pad, pad),), dimension_numbers=('NWC', 'WIO', 'NWC'),
                                    feature_group_count=c)


def chunked_gated_delta(q, k, v, g, beta):
    B, T, H, DK = q.shape
    DV = v.shape[-1]
    N = T // GDN_CHUNK

    def to_chunks(a):
        a = a.reshape((B, N, GDN_CHUNK, H) + a.shape[3:])
        return jnp.moveaxis(a, (1, 3), (0, 2))

    q, k, v, g, beta = (to_chunks(a) for a in (q, k, v, g, beta))
    g = jnp.cumsum(g, axis=-1)
    idx = jnp.arange(GDN_CHUNK)
    lower = idx[:, None] >= idx[None, :]
    strict = idx[:, None] > idx[None, :]
    decay = jnp.exp(jnp.where(lower, g[..., :, None] - g[..., None, :], -jnp.inf))
    kb = k * beta[..., None]
    L = jnp.where(strict, jnp.einsum('nbhid,nbhjd->nbhij', kb, k) * decay, 0.0)
    rhs = jnp.concatenate([v * beta[..., None], kb * jnp.exp(g)[..., None]], axis=-1)
    uw = lax.linalg.triangular_solve(L, rhs, left_side=True, lower=True, unit_diagonal=True)
    u, w = uw[..., :DV], uw[..., DV:]
    a_qk = jnp.einsum('nbhid,nbhjd->nbhij', q, k) * decay
    q_dec = q * jnp.exp(g)[..., None]
    k_dec = k * jnp.exp(g[..., -1:] - g)[..., None]
    g_end = jnp.exp(g[..., -1])

    def step(S, xs):
        qd, kd, uu, ww, aa, ge = xs
        v_new = uu - jnp.einsum('bhck,bhkv->bhcv', ww, S)
        o = jnp.einsum('bhck,bhkv->bhcv', qd, S) + jnp.einsum('bhij,bhjv->bhiv', aa, v_new)
        S = S * ge[..., None, None] + jnp.einsum('bhck,bhcv->bhkv', kd, v_new)
        return S, o

    S0 = jnp.zeros((B, H, DK, DV), jnp.float32)
    _, o = lax.scan(step, S0, (q_dec, k_dec, u, w, a_qk, g_end))
    return jnp.moveaxis(o, (0, 2), (1, 3)).reshape(B, T, H, DV)


def gdn_branch(qkv, z, b_pre, a_pre, conv_w, a_log, dt_bias, norm_w):
    B, T, _ = qkv.shape
    H = GDN_HEADS
    u = jax.nn.silu(short_conv(qkv, conv_w)).astype(jnp.float32)
    q = l2_normalize(u[..., :GDN_QK_W].reshape(B, T, H, GDN_DK)) * (GDN_DK ** -0.5)
    k = l2_normalize(u[..., GDN_QK_W:2 * GDN_QK_W].reshape(B, T, H, GDN_DK))
    v = u[..., 2 * GDN_QK_W:].reshape(B, T, H, GDN_DV)
    beta = jax.nn.sigmoid(b_pre.astype(jnp.float32)).reshape(B, T, 2, H)
    g = -jnp.exp(a_log.astype(jnp.float32)) * jax.nn.softplus(
        a_pre.astype(jnp.float32).reshape(B, T, 2, H) + dt_bias.astype(jnp.float32))
    flip = lambda a: jnp.flip(a, axis=1)
    q2 = jnp.concatenate([q, flip(q)], axis=0)
    k2 = jnp.concatenate([k, flip(k)], axis=0)
    v2 = jnp.concatenate([v, flip(v)], axis=0)
    g2 = jnp.concatenate([g[:, :, 0], flip(g[:, :, 1])], axis=0)
    b2 = jnp.concatenate([beta[:, :, 0], flip(beta[:, :, 1])], axis=0)
    o2 = chunked_gated_delta(q2, k2, v2, g2, b2)
    o = o2[:B] + flip(o2[B:])
    o = o * lax.rsqrt(jnp.mean(o * o, axis=-1, keepdims=True) + RMS_EPS) * norm_w.astype(jnp.float32)
    o = o * jax.nn.silu(z.astype(jnp.float32).reshape(B, T, H, GDN_DV))
    return o.reshape(B, T, GDN_V_W).astype(qkv.dtype)


def partial_rope(x, pos):
    half = ROPE_DIMS // 2
    inv_freq = jnp.power(ROPE_THETA, -jnp.arange(half, dtype=jnp.float32) * 2.0 / ROPE_DIMS)
    ang = pos[:, None] * inv_freq[None, :]
    cos = jnp.cos(ang)[:, None, :]
    sin = jnp.sin(ang)[:, None, :]
    xf = x.astype(jnp.float32)
    x1, x2, rest = xf[..., :half], xf[..., half:ROPE_DIMS], xf[..., ROPE_DIMS:]
    return jnp.concatenate([x1 * cos - x2 * sin, x2 * cos + x1 * sin, rest], axis=-1)


def dilated_window_attention(q, k, v, dilation, span):
    B, T, H, D = q.shape
    M = T // dilation
    nb = -(-M // span)
    Mp = nb * span

    def subseq(a):
        a = a.reshape(B, M, dilation, H, D).transpose(0, 2, 1, 3, 4)
        return jnp.pad(a, ((0, 0), (0, 0), (0, Mp - M), (0, 0), (0, 0)))

    def windows(a):
        a = jnp.pad(subseq(a), ((0, 0), (0, 0), (span, span), (0, 0), (0, 0)))
        a = a.reshape(B, dilation, nb + 2, span, H, D)
        return jnp.concatenate([a[:, :, :-2], a[:, :, 1:-1], a[:, :, 2:]], axis=3)

    qs = subseq(q).reshape(B, dilation, nb, span, H, D)
    kw, vw = windows(k), windows(v)
    mq = jnp.arange(nb)[:, None] * span + jnp.arange(span)[None, :]
    mk = jnp.arange(nb)[:, None] * span - span + jnp.arange(3 * span)[None, :]
    valid = ((mk[:, None, :] >= 0) & (mk[:, None, :] < M)
             & (jnp.abs(mq[:, :, None] - mk[:, None, :]) <= span))
    s = jnp.einsum('brnqhd,brnkhd->brnhqk', qs, kw) * (D ** -0.5)
    s = jnp.where(valid[:, None], s, NEG_BIG)
    m = jnp.max(s, axis=-1, keepdims=True)
    p = jnp.exp(s - m)
    l = jnp.sum(p, axis=-1, keepdims=True)
    lse = jnp.swapaxes((m + jnp.log(l))[..., 0], -1, -2)
    o = jnp.einsum('brnhqk,brnkhd->brnqhd', p, vw) / jnp.swapaxes(l[..., 0], -1, -2)[..., None]
    o = o.reshape(B, dilation, Mp, H, D)[:, :, :M].transpose(0, 2, 1, 3, 4).reshape(B, T, H, D)
    lse = lse.reshape(B, dilation, Mp, H)[:, :, :M].transpose(0, 2, 1, 3).reshape(B, T, H)
    return o, lse


def dilated_attention_branch(qkv):
    B, T, _ = qkv.shape
    q = qkv[..., :ATT_W].reshape(B, T, ATT_HEADS, ATT_HEAD_DIM)
    k = qkv[..., ATT_W:2 * ATT_W].reshape(B, T, ATT_HEADS, ATT_HEAD_DIM)
    v = qkv[..., 2 * ATT_W:].reshape(B, T, ATT_HEADS, ATT_HEAD_DIM).astype(jnp.float32)
    pos = jnp.arange(T, dtype=jnp.float32)
    q = partial_rope(q, pos)
    k = partial_rope(k, pos)
    outs, lses = [], []
    for gi, (window, dilation) in enumerate(ATT_GROUPS):
        hs = slice(gi * ATT_HEADS_PER_GROUP, (gi + 1) * ATT_HEADS_PER_GROUP)
        o, lse = dilated_window_attention(q[:, :, hs], k[:, :, hs], v[:, :, hs], dilation, window // 2 // dilation)
        outs.append(o)
        lses.append(lse)
    wts = jax.nn.softmax(jnp.stack(lses, axis=0), axis=0)
    o = jnp.sum(wts[..., None] * jnp.stack(outs, axis=0), axis=0)
    return o.reshape(B, T, ATT_OUT_W).astype(qkv.dtype)


def expert_choice_ffn(h, w_router, w_gate, w_up, w_down):
    B, T, D = h.shape
    cap = EC_CAPACITY_FACTOR * T // N_EXPERTS
    aff = jax.nn.softmax(h.astype(jnp.float32) @ w_router.astype(jnp.float32), axis=-1)
    gate, idx = lax.top_k(jnp.swapaxes(aff, 1, 2), cap)
    xe = jax.vmap(lambda hb, ib: hb[ib])(h, idx)
    a = jnp.einsum('becd,edf->becf', xe, w_gate)
    u = jnp.einsum('becd,edf->becf', xe, w_up)
    y = jnp.einsum('becf,efd->becd', jax.nn.silu(a) * u, w_down)
    y = y * gate[..., None].astype(y.dtype)
    return jax.vmap(lambda yb, ib: jnp.zeros((T, D), y.dtype).at[ib.reshape(-1)].add(yb.reshape(-1, D)))(y, idx)


def setup_inputs(seed: int = 0) -> dict:
    key = jax.random.key(seed)
    ks = jax.random.split(key, 20)
    f32 = jnp.float32

    def dense(k, shape, fan_in):
        return jax.random.normal(k, shape, f32) * (fan_in ** -0.5)

    def gain(k, shape):
        return 1.0 + 0.05 * jax.random.normal(k, shape, f32)

    x = jax.random.normal(ks[0], (BATCH, SEQ, D_MODEL), f32)
    dt = jnp.exp(jax.random.uniform(ks[5], (DEPTH, 2, GDN_HEADS), f32, np.log(1e-3), np.log(1e-1)))
    return {
        "x": x,
        "norm_mix": gain(ks[1], (DEPTH, D_MODEL)),
        "w_in": dense(ks[2], (DEPTH, D_MODEL, P_IN), D_MODEL),
        "conv_w": dense(ks[3], (DEPTH, CONV_WIDTH, GDN_CONV_CH), CONV_WIDTH),
        "a_log": jnp.log(jax.random.uniform(ks[4], (DEPTH, 2, GDN_HEADS), f32, 1.0, 16.0)),
        "dt_bias": dt + jnp.log(-jnp.expm1(-dt)),
        "gdn_norm": gain(ks[6], (DEPTH, GDN_DV)),
        "w_branch_a": dense(ks[7], (DEPTH, GDN_V_W, D_MODEL), GDN_V_W),
        "w_branch_b": dense(ks[8], (DEPTH, ATT_OUT_W, D_MODEL), ATT_OUT_W),
        "w_out": dense(ks[9], (DEPTH, D_MODEL, D_MODEL), D_MODEL),
        "norm_ffn": gain(ks[10], (DEPTH, D_MODEL)),
        "w_router": dense(ks[11], (DEPTH, D_MODEL, N_EXPERTS), D_MODEL),
        "w_expert_gate": dense(ks[12], (DEPTH, N_EXPERTS, D_MODEL, EXPERT_FF), D_MODEL),
        "w_expert_up": dense(ks[13], (DEPTH, N_EXPERTS, D_MODEL, EXPERT_FF), D_MODEL),
        "w_expert_down": dense(ks[14], (DEPTH, N_EXPERTS, EXPERT_FF, D_MODEL), EXPERT_FF),
        "norm_final": gain(ks[15], (D_MODEL,)),
    }


def reference(x, norm_mix, w_in, conv_w, a_log, dt_bias, gdn_norm, w_branch_a, w_branch_b, w_out,
              norm_ffn, w_router, w_expert_gate, w_expert_up, w_expert_down, norm_final):
    for layer in range(DEPTH):
        h = rms_norm(x, norm_mix[layer])
        proj = h @ w_in[layer]
        qkv_a, z_a, b_a, a_a, qkv_b, gates = jnp.split(
            proj, [OFF_Z, OFF_BETA, OFF_DECAY, OFF_ATT, OFF_GATE], axis=-1)
        y_a = gdn_branch(qkv_a, z_a, b_a, a_a, conv_w[layer], a_log[layer], dt_bias[layer], gdn_norm[layer])
        y_b = dilated_attention_branch(qkv_b)
        g = jax.nn.sigmoid(gates.astype(jnp.float32)).astype(x.dtype)
        mixed = g[..., :D_MODEL] * (y_a @ w_branch_a[layer]) + g[..., D_MODEL:] * (y_b @ w_branch_b[layer])
        x = x + mixed @ w_out[layer]
        h = rms_norm(x, norm_ffn[layer])
        x = x + expert_choice_ffn(h, w_router[layer], w_expert_gate[layer], w_expert_up[layer], w_expert_down[layer])
    return rms_norm(x, norm_final)
```

```python
import functools

import numpy as np
import jax
import jax.numpy as jnp
from jax import lax
from jax.experimental import pallas as pl
from jax.experimental.pallas import tpu as pltpu

F32 = jnp.float32
BF16 = jnp.bfloat16

D_MODEL = 1024
GDN_HEADS = 4
GDN_DK = 128
GDN_DV = 128
CONV_WIDTH = 5
ATT_GROUPS = ((128, 1), (512, 4), (2048, 16))
ATT_HPG = 4
ATT_HEAD_DIM = 64
ATT_SPAN = 64
ROPE_THETA = 500000.0
ROPE_DIMS = ATT_HEAD_DIM // 4
N_EXPERTS = 16
EXPERT_FF = 1024
EC_CAPACITY_FACTOR = 2
RMS_EPS = 1e-6
NEG_BIG = -1e30

GDN_W = GDN_HEADS * GDN_DK
GDN_CONV_CH = 3 * GDN_W
ATT_GW = ATT_HPG * ATT_HEAD_DIM
ATT_W = len(ATT_GROUPS) * ATT_GW

C_QKVA = 0
C_Z = C_QKVA + GDN_CONV_CH
C_GATE = C_Z + GDN_W
C_ATT = C_GATE + 2 * D_MODEL
C_BD = C_ATT + 3 * ATT_W
PROJ_W = 6656
PROJ_TN = 512

LANES = 128
VMEM_LIMIT = 56 * 1024 * 1024


def _cparams(sem):
    return pltpu.CompilerParams(dimension_semantics=sem, vmem_limit_bytes=VMEM_LIMIT)


def _proj_kernel(x_ref, nw_ref, w_ref, rc_ref, rs_ref, o_ref, h_ref, *, rope_lo, rope_hi):
    j = pl.program_id(1)

    @pl.when(j == 0)
    def _():
        x = x_ref[...]
        ms = jnp.mean(x * x, axis=-1, keepdims=True)
        h_ref[...] = (x * lax.rsqrt(ms + RMS_EPS) * nw_ref[...]).astype(BF16)

    acc = jnp.dot(h_ref[...], w_ref[...], preferred_element_type=F32)
    is_rope = jnp.logical_and(j >= rope_lo, j < rope_hi)

    @pl.when(is_rope)
    def _():
        tn = acc.shape[1]
        reps = tn // LANES
        cos = jnp.concatenate([rc_ref[...]] * reps, axis=1)
        sin = jnp.concatenate([rs_ref[...]] * reps, axis=1)
        lane = lax.broadcasted_iota(jnp.int32, acc.shape, 1) % ATT_HEAD_DIM
        half = ROPE_DIMS // 2
        partner = jnp.where(lane < half, pltpu.roll(acc, tn - half, 1), pltpu.roll(acc, half, 1))
        o_ref[...] = acc * cos + partner * sin

    @pl.when(jnp.logical_not(is_rope))
    def _():
        o_ref[...] = acc


def _proj(x2d, norm_w, w_all, rope_cos, rope_sin, seq):
    n = x2d.shape[0]
    tm = min(1024, seq)
    tpb = seq // tm
    kern = functools.partial(_proj_kernel, rope_lo=C_ATT // PROJ_TN, rope_hi=(C_ATT + 2 * ATT_W) // PROJ_TN)
    return pl.pallas_call(
        kern,
        grid=(n // tm, PROJ_W // PROJ_TN),
        in_specs=[
            pl.BlockSpec((tm, D_MODEL), lambda i, j: (i, 0)),
            pl.BlockSpec((1, D_MODEL), lambda i, j: (0, 0)),
            pl.BlockSpec((D_MODEL, PROJ_TN), lambda i, j: (0, j)),
            pl.BlockSpec((tm, LANES), lambda i, j: (i % tpb, 0)),
            pl.BlockSpec((tm, LANES), lambda i, j: (i % tpb, 0)),
        ],
        out_specs=pl.BlockSpec((tm, PROJ_TN), lambda i, j: (i, j)),
        out_shape=jax.ShapeDtypeStruct((n, PROJ_W), F32),
        scratch_shapes=[pltpu.VMEM((tm, D_MODEL), BF16)],
        compiler_params=_cparams(("parallel", "arbitrary")),
        name="proj",
    )(x2d, norm_w, w_all, rope_cos, rope_sin)


def _rope_tables(seq):
    half = ROPE_DIMS // 2
    inv_freq = jnp.power(ROPE_THETA, -jnp.arange(half, dtype=F32) * 2.0 / ROPE_DIMS)
    ang = jnp.arange(seq, dtype=F32)[:, None] * inv_freq[None, :]
    cos, sin = jnp.cos(ang), jnp.sin(ang)
    rest = ATT_HEAD_DIM - ROPE_DIMS
    cos_h = jnp.concatenate([cos, cos, jnp.ones((seq, rest), F32)], axis=1)
    sin_h = jnp.concatenate([-sin, sin, jnp.zeros((seq, rest), F32)], axis=1)
    reps = LANES // ATT_HEAD_DIM
    return jnp.tile(cos_h, (1, reps)), jnp.tile(sin_h, (1, reps))


def _proj_weights(w_in):
    off_z = GDN_CONV_CH
    off_beta = off_z + GDN_W
    off_att = off_beta + 4 * GDN_HEADS
    off_gate = off_att + 3 * ATT_W
    qkva = w_in[:, :off_z]
    z = w_in[:, off_z:off_beta]
    bd = w_in[:, off_beta:off_att]
    att = w_in[:, off_att:off_gate]
    gates = w_in[:, off_gate:]
    att = jnp.concatenate([att[:, :ATT_W] * (ATT_HEAD_DIM ** -0.5), att[:, ATT_W:]], axis=1)
    pad = jnp.zeros((w_in.shape[0], PROJ_W - C_BD - bd.shape[1]), w_in.dtype)
    return jnp.concatenate([qkva, z, gates, att, bd, pad], axis=1).astype(BF16)


AUX_BETA = 0
AUX_GC = 8
AUX_EG = 16
AUX_EK = 24
AUX_GE = 32
HALO = 8


def _gdn_prep_kernel(xp_ref, xc_ref, xn_ref, bd_ref, cw_ref, alog_ref, dtb_ref,
                     qkv_ref, aux_ref, gcr_ref, xf_ref, *, chunk):
    i = pl.program_id(1)
    nblk = pl.num_programs(1)
    tp = xc_ref.shape[1]
    prev = jnp.where(i > 0, xp_ref[0], 0.0)
    nxt = jnp.where(i < nblk - 1, xn_ref[0], 0.0)
    xf_ref[0:HALO, :] = prev
    xf_ref[HALO:HALO + tp, :] = xc_ref[0]
    xf_ref[HALO + tp:, :] = nxt
    pad = (CONV_WIDTH - 1) // 2
    conv = None
    for j in range(CONV_WIDTH):
        term = xf_ref[pl.ds(HALO - pad + j, tp), :] * cw_ref[j:j + 1, :]
        conv = term if conv is None else conv + term
    u = conv * (1.0 / (1.0 + jnp.exp(-conv)))
    for h in range(2 * GDN_HEADS):
        sl = slice(h * GDN_DK, (h + 1) * GDN_DK)
        uh = u[:, sl]
        scale = lax.rsqrt(jnp.sum(uh * uh, axis=-1, keepdims=True) + 1e-6)
        if h < GDN_HEADS:
            scale = scale * (GDN_DK ** -0.5)
        qkv_ref[0, :, sl] = uh * scale
    qkv_ref[0, :, 2 * GDN_W:] = u[:, 2 * GDN_W:]

    nh2 = 2 * GDN_HEADS
    bd = bd_ref[0]
    lane = lax.broadcasted_iota(jnp.int32, bd.shape, 1)
    row = lax.broadcasted_iota(jnp.int32, bd.shape, 0) % chunk
    beta = 1.0 / (1.0 + jnp.exp(-bd))
    sp_in = bd + dtb_ref[...]
    softplus = jnp.maximum(sp_in, 0.0) + jnp.log(1.0 + jnp.exp(-jnp.abs(sp_in)))
    g = -jnp.exp(alog_ref[...]) * softplus
    g = jnp.where(jnp.logical_and(lane >= nh2, lane < 2 * nh2), g, 0.0)
    pre = g
    suf = g
    s = 1
    while s < chunk:
        pre = pre + jnp.where(row >= s, pltpu.roll(pre, s, 0), 0.0)
        suf = suf + jnp.where(row < chunk - s, pltpu.roll(suf, tp - s, 0), 0.0)
        s *= 2
    total = pre + suf - g
    is_fwd = lane < nh2 + GDN_HEADS
    gc = jnp.where(is_fwd, pre, suf)
    eg = jnp.exp(gc)
    ek = jnp.exp(total - gc)
    ge = jnp.exp(total)
    aux = jnp.where(lane < nh2, beta, 0.0)
    aux = aux + gc
    aux = aux + jnp.where(jnp.logical_and(lane >= AUX_EG, lane < AUX_EG + nh2), pltpu.roll(eg, AUX_EG - AUX_GC, 1), 0.0)
    aux = aux + jnp.where(jnp.logical_and(lane >= AUX_EK, lane < AUX_EK + nh2), pltpu.roll(ek, AUX_EK - AUX_GC, 1), 0.0)
    aux = aux + jnp.where(jnp.logical_and(lane >= AUX_GE, lane < AUX_GE + nh2), pltpu.roll(ge, AUX_GE - AUX_GC, 1), 0.0)
    aux_ref[0] = aux
    gct = gc.T
    for c in range(tp // chunk):
        gcr_ref[0, c] = gct[AUX_GC:AUX_GC + nh2, c * chunk:(c + 1) * chunk]


def _gdn_prep(proj3, conv_w, a_log, dt_bias, chunk):
    b, t, _ = proj3.shape
    tp = min(512, t)
    nblk = t // tp
    nh2 = 2 * GDN_HEADS
    alog_row = jnp.zeros((1, LANES), F32).at[0, nh2:2 * nh2].set(a_log.reshape(-1))
    dtb_row = jnp.zeros((1, LANES), F32).at[0, nh2:2 * nh2].set(dt_bias.reshape(-1))
    hb = tp // HALO
    kern = functools.partial(_gdn_prep_kernel, chunk=chunk)
    return pl.pallas_call(
        kern,
        grid=(b, nblk),
        in_specs=[
            pl.BlockSpec((1, HALO, GDN_CONV_CH), lambda bi, i: (bi, jnp.maximum(i * hb - 1, 0), 0)),
            pl.BlockSpec((1, tp, GDN_CONV_CH), lambda bi, i: (bi, i, 0)),
            pl.BlockSpec((1, HALO, GDN_CONV_CH), lambda bi, i: (bi, jnp.minimum((i + 1) * hb, t // HALO - 1), 0)),
            pl.BlockSpec((1, tp, LANES), lambda bi, i: (bi, i, C_BD // LANES)),
            pl.BlockSpec((CONV_WIDTH, GDN_CONV_CH), lambda bi, i: (0, 0)),
            pl.BlockSpec((1, LANES), lambda bi, i: (0, 0)),
            pl.BlockSpec((1, LANES), lambda bi, i: (0, 0)),
        ],
        out_specs=[
            pl.BlockSpec((1, tp, GDN_CONV_CH), lambda bi, i: (bi, i, 0)),
            pl.BlockSpec((1, tp, LANES), lambda bi, i: (bi, i, 0)),
            pl.BlockSpec((1, tp // chunk, nh2, chunk), lambda bi, i: (bi, i, 0, 0)),
        ],
        out_shape=[
            jax.ShapeDtypeStruct((b, t, GDN_CONV_CH), F32),
            jax.ShapeDtypeStruct((b, t, LANES), F32),
            jax.ShapeDtypeStruct((b, t // chunk, nh2, chunk), F32),
        ],
        scratch_shapes=[pltpu.VMEM((tp + 2 * HALO, GDN_CONV_CH), F32)],
        compiler_params=_cparams(("parallel", "parallel")),
        name="gdn_prep",
    )(proj3, proj3, proj3, proj3, conv_w, alog_row, dtb_row)


_DN_T = (((1,), (1,)), ((), ()))
_DN_L = (((0,), (0,)), ((), ()))


def _gdn_unit(q, k, v, aux, gcr, s_prev, h, d, chunk):
    col = d * GDN_HEADS + h
    beta = aux[:, AUX_BETA + col:AUX_BETA + col + 1]
    gc = aux[:, AUX_GC + col:AUX_GC + col + 1]
    eg = aux[:, AUX_EG + col:AUX_EG + col + 1]
    ek = aux[:, AUX_EK + col:AUX_EK + col + 1]
    ge = aux[0:1, AUX_GE + col:AUX_GE + col + 1]
    ri = lax.broadcasted_iota(jnp.int32, (chunk, chunk), 0)
    ci = lax.broadcasted_iota(jnp.int32, (chunk, chunk), 1)
    if d == 0:
        incl, strict = ri >= ci, ri > ci
    else:
        incl, strict = ri <= ci, ri < ci
    decay = jnp.where(incl, jnp.exp(jnp.minimum(gc - gcr, 0.0)), 0.0)
    kb = k * beta
    kf = k.astype(BF16)
    kk = lax.dot_general(kb.astype(BF16), kf, _DN_T, preferred_element_type=F32)
    lmat = jnp.where(strict, kk * decay, 0.0)
    qk = lax.dot_general(q.astype(BF16), kf, _DN_T, preferred_element_type=F32)
    amat = qk * decay
    tinv = jnp.where(ri == ci, 1.0, 0.0) - lmat
    lp = lmat
    p = 2
    while p < chunk:
        lp = jnp.dot(lp, lp, preferred_element_type=F32)
        tinv = tinv + jnp.dot(tinv, lp, preferred_element_type=F32)
        p *= 2
    rhs = jnp.concatenate([v * beta, kb * eg], axis=1)
    uw = jnp.dot(tinv, rhs, preferred_element_type=F32)
    u, w = uw[:, :GDN_DV], uw[:, GDN_DV:]
    v_new = u - jnp.dot(w, s_prev, preferred_element_type=F32)
    o = jnp.dot(q * eg, s_prev, preferred_element_type=F32) + jnp.dot(amat, v_new, preferred_element_type=F32)
    s_new = s_prev * ge + lax.dot_general(k * ek, v_new, _DN_L, preferred_element_type=F32)
    return o, s_new


def _gdn_scan_kernel(qkvf_ref, qkvb_ref, auxf_ref, auxb_ref, gcrf_ref, gcrb_ref,
                     of_ref, ob_ref, s_ref, *, chunk):
    @pl.when(pl.program_id(1) == 0)
    def _():
        s_ref[...] = jnp.zeros_like(s_ref)

    nc = qkvf_ref.shape[1] // chunk
    dirs = ((qkvf_ref, auxf_ref, gcrf_ref, of_ref), (qkvb_ref, auxb_ref, gcrb_ref, ob_ref))

    def body(c, carry):
        for d, (qkv_ref, aux_ref, gcr_ref, o_ref) in enumerate(dirs):
            cc = c if d == 0 else nc - 1 - c
            r0 = pl.multiple_of(cc * chunk, chunk)
            aux = aux_ref[0, pl.ds(r0, chunk), :]
            gcr_all = gcr_ref[0, cc]
            for h in range(GDN_HEADS):
                q = qkv_ref[0, pl.ds(r0, chunk), h * GDN_DK:(h + 1) * GDN_DK]
                k = qkv_ref[0, pl.ds(r0, chunk), GDN_W + h * GDN_DK:GDN_W + (h + 1) * GDN_DK]
                v = qkv_ref[0, pl.ds(r0, chunk), 2 * GDN_W + h * GDN_DV:2 * GDN_W + (h + 1) * GDN_DV]
                si = d * GDN_HEADS + h
                gcr = gcr_all[si:si + 1, :]
                o, s_new = _gdn_unit(q, k, v, aux, gcr, s_ref[si], h, d, chunk)
                s_ref[si] = s_new
                o_ref[0, pl.ds(r0, chunk), h * GDN_DV:(h + 1) * GDN_DV] = o
        return carry

    lax.fori_loop(0, nc, body, 0)


def _gdn_scan(qkv, aux, gcr, chunk):
    b, t, _ = qkv.shape
    tb = min(512, t)
    nb = t // tb
    nh2 = 2 * GDN_HEADS
    kern = functools.partial(_gdn_scan_kernel, chunk=chunk)
    fwd = lambda bi, i: (bi, i, 0)
    bwd = lambda bi, i: (bi, nb - 1 - i, 0)
    return pl.pallas_call(
        kern,
        grid=(b, nb),
        in_specs=[
            pl.BlockSpec((1, tb, GDN_CONV_CH), fwd),
            pl.BlockSpec((1, tb, GDN_CONV_CH), bwd),
            pl.BlockSpec((1, tb, LANES), fwd),
            pl.BlockSpec((1, tb, LANES), bwd),
            pl.BlockSpec((1, tb // chunk, nh2, chunk), lambda bi, i: (bi, i, 0, 0)),
            pl.BlockSpec((1, tb // chunk, nh2, chunk), lambda bi, i: (bi, nb - 1 - i, 0, 0)),
        ],
        out_specs=[pl.BlockSpec((1, tb, GDN_W), fwd), pl.BlockSpec((1, tb, GDN_W), bwd)],
        out_shape=[jax.ShapeDtypeStruct((b, t, GDN_W), F32)] * 2,
        scratch_shapes=[pltpu.VMEM((nh2, GDN_DK, GDN_DV), F32)],
        compiler_params=_cparams(("parallel", "arbitrary")),
        name="gdn_scan",
    )(qkv, qkv, aux, aux, gcr, gcr)


ATT_QT = 128


def _attn_kernel(q_ref, kp_ref, kc_ref, kn_ref, vp_ref, vc_ref, vn_ref, o_ref, lse_ref, kf_ref, vf_ref, *, m_total):
    i = pl.program_id(2)
    ts = q_ref.shape[1]
    kf_ref[0:ATT_SPAN, :] = kp_ref[0].astype(BF16)
    kf_ref[ATT_SPAN:ATT_SPAN + ts, :] = kc_ref[0].astype(BF16)
    kf_ref[ATT_SPAN + ts:, :] = kn_ref[0].astype(BF16)
    vf_ref[0:ATT_SPAN, :] = vp_ref[0].astype(BF16)
    vf_ref[ATT_SPAN:ATT_SPAN + ts, :] = vc_ref[0].astype(BF16)
    vf_ref[ATT_SPAN + ts:, :] = vn_ref[0].astype(BF16)
    nk = ATT_QT + 2 * ATT_SPAN
    qi = lax.broadcasted_iota(jnp.int32, (ATT_QT, nk), 0)
    ki = lax.broadcasted_iota(jnp.int32, (ATT_QT, nk), 1)
    rel = ki - qi
    band = jnp.logical_and(rel >= 0, rel <= 2 * ATT_SPAN)
    head_of_lane = lax.broadcasted_iota(jnp.int32, (ATT_QT, ATT_GW), 1) // ATT_HEAD_DIM
    for j in range(ts // ATT_QT):
        q = q_ref[0, j * ATT_QT:(j + 1) * ATT_QT, :]
        kt = kf_ref[j * ATT_QT:j * ATT_QT + nk, :]
        vt = vf_ref[j * ATT_QT:j * ATT_QT + nk, :]
        mk = i * ts + j * ATT_QT - ATT_SPAN + ki
        valid = jnp.logical_and(band, jnp.logical_and(mk >= 0, mk < m_total))
        o_acc = jnp.zeros((ATT_QT, ATT_GW), F32)
        lse_acc = jnp.zeros((ATT_QT, ATT_GW), F32)
        for h in range(ATT_HPG):
            in_head = head_of_lane == h
            qh = jnp.where(in_head, q, 0.0).astype(BF16)
            s = lax.dot_general(qh, kt, _DN_T, preferred_element_type=F32)
            s = jnp.where(valid, s, NEG_BIG)
            mx = jnp.max(s, axis=-1, keepdims=True)
            p = jnp.exp(s - mx)
            l = jnp.sum(p, axis=-1, keepdims=True)
            pv = jnp.dot(p.astype(BF16), vt, preferred_element_type=F32)
            o_acc = jnp.where(in_head, pv / l, o_acc)
            lse_acc = jnp.where(in_head, mx + jnp.log(l), lse_acc)
        o_ref[0, j * ATT_QT:(j + 1) * ATT_QT, :] = o_acc
        lse_ref[0, j * ATT_QT:(j + 1) * ATT_QT, :] = lse_acc


def _attn_group(proj3, gi, dilation):
    b, t, _ = proj3.shape
    r = dilation
    m = t // r
    assert m % ATT_SPAN == 0
    ts = min(512, m)
    nblk = m // ts
    pv = proj3.reshape(b, m, r * PROJ_W)
    cpb = PROJ_W // ATT_GW
    cq = C_ATT // ATT_GW + gi
    ck = cq + ATT_W // ATT_GW
    cv = ck + ATT_W // ATT_GW
    hb = ts // ATT_SPAN
    last = m // ATT_SPAN - 1

    def center(c):
        return pl.BlockSpec((1, ts, ATT_GW), lambda bi, s, i: (bi, i, s * cpb + c))

    def prev(c):
        return pl.BlockSpec((1, ATT_SPAN, ATT_GW), lambda bi, s, i: (bi, jnp.maximum(i * hb - 1, 0), s * cpb + c))

    def nxt(c):
        return pl.BlockSpec((1, ATT_SPAN, ATT_GW), lambda bi, s, i: (bi, jnp.minimum((i + 1) * hb, last), s * cpb + c))

    out_spec = pl.BlockSpec((1, ts, ATT_GW), lambda bi, s, i: (bi, i, s))
    kern = functools.partial(_attn_kernel, m_total=m)
    o, lse = pl.pallas_call(
        kern,
        grid=(b, r, nblk),
        in_specs=[center(cq), prev(ck), center(ck), nxt(ck), prev(cv), center(cv), nxt(cv)],
        out_specs=[out_spec, out_spec],
        out_shape=[jax.ShapeDtypeStruct((b, m, r * ATT_GW), F32)] * 2,
        scratch_shapes=[pltpu.VMEM((ts + 2 * ATT_SPAN, ATT_GW), BF16)] * 2,
        compiler_params=_cparams(("parallel", "parallel", "parallel")),
        name=f"attn_d{dilation}",
    )(pv, pv, pv, pv, pv, pv, pv)
    return o.reshape(b, t, ATT_GW), lse.reshape(b, t, ATT_GW)


def _merge_kernel(of_ref, ob_ref, z_ref, ga_ref, gb_ref, o0_ref, o1_ref, o2_ref, l0_ref, l1_ref, l2_ref,
                  x_ref, gn_ref, wa_ref, wb_ref, wo_ref, nf_ref, wr_ref,
                  x1_ref, h2_ref, afft_ref):
    o = of_ref[...] + ob_ref[...]
    z = z_ref[...]
    ya_parts = []
    for h in range(GDN_HEADS):
        sl = slice(h * GDN_DV, (h + 1) * GDN_DV)
        oh = o[:, sl]
        oh = oh * lax.rsqrt(jnp.mean(oh * oh, axis=-1, keepdims=True) + RMS_EPS) * gn_ref[...]
        zh = z[:, sl]
        ya_parts.append(oh * (zh * (1.0 / (1.0 + jnp.exp(-zh)))))
    y_a = jnp.concatenate(ya_parts, axis=1).astype(BF16)
    l0, l1, l2 = l0_ref[...], l1_ref[...], l2_ref[...]
    lm = jnp.maximum(jnp.maximum(l0, l1), l2)
    e0, e1, e2 = jnp.exp(l0 - lm), jnp.exp(l1 - lm), jnp.exp(l2 - lm)
    y_b = ((e0 * o0_ref[...] + e1 * o1_ref[...] + e2 * o2_ref[...]) / (e0 + e1 + e2)).astype(BF16)
    ga = 1.0 / (1.0 + jnp.exp(-ga_ref[...]))
    gb = 1.0 / (1.0 + jnp.exp(-gb_ref[...]))
    mixed = (ga * jnp.dot(y_a, wa_ref[...], preferred_element_type=F32)
             + gb * jnp.dot(y_b, wb_ref[...], preferred_element_type=F32))
    x1 = x_ref[...] + jnp.dot(mixed.astype(BF16), wo_ref[...], preferred_element_type=F32)
    x1_ref[...] = x1
    h2 = (x1 * lax.rsqrt(jnp.mean(x1 * x1, axis=-1, keepdims=True) + RMS_EPS) * nf_ref[...]).astype(BF16)
    h2_ref[...] = h2
    logits = jnp.dot(h2, wr_ref[...], preferred_element_type=F32)
    lane = lax.broadcasted_iota(jnp.int32, logits.shape, 1)
    logits = jnp.where(lane < N_EXPERTS, logits, NEG_BIG)
    lmax = jnp.max(logits, axis=-1, keepdims=True)
    ex = jnp.exp(logits - lmax)
    aff = ex / jnp.sum(ex, axis=-1, keepdims=True)
    afft_ref[0] = aff.T[:N_EXPERTS, :]


def _merge(of, ob, proj, att, x2d, gdn_norm, wa, wb, wo, norm_ffn, w_router, seq):
    n = x2d.shape[0]
    tm = min(512, seq)
    tpb = seq // tm
    (o0, l0), (o1, l1), (o2, l2) = att
    wr = jnp.zeros((D_MODEL, LANES), BF16).at[:, :N_EXPERTS].set(w_router.astype(BF16))
    row = lambda w: pl.BlockSpec((tm, w), lambda i: (i, 0))
    full = lambda a: pl.BlockSpec(a.shape, lambda i: (0,) * a.ndim)
    gn = gdn_norm.reshape(1, GDN_DV)
    nf = norm_ffn.reshape(1, D_MODEL)
    in_specs = [
        row(GDN_W), row(GDN_W),
        pl.BlockSpec((tm, GDN_W), lambda i: (i, C_Z // GDN_W)),
        pl.BlockSpec((tm, D_MODEL), lambda i: (i, C_GATE // D_MODEL)),
        pl.BlockSpec((tm, D_MODEL), lambda i: (i, C_GATE // D_MODEL + 1)),
        row(ATT_GW), row(ATT_GW), row(ATT_GW), row(ATT_GW), row(ATT_GW), row(ATT_GW),
        row(D_MODEL), full(gn), full(wa), full(wb), full(wo), full(nf), full(wr),
    ]
    return pl.pallas_call(
        _merge_kernel,
        grid=(n // tm,),
        in_specs=in_specs,
        out_specs=[
            row(D_MODEL), row(D_MODEL),
            pl.BlockSpec((1, N_EXPERTS, tm), lambda i: (i // tpb, 0, i % tpb)),
        ],
        out_shape=[
            jax.ShapeDtypeStruct((n, D_MODEL), F32),
            jax.ShapeDtypeStruct((n, D_MODEL), BF16),
            jax.ShapeDtypeStruct((n // seq, N_EXPERTS, seq), F32),
        ],
        compiler_params=_cparams(("parallel",)),
        name="merge",
    )(of, ob, proj, proj, proj, o0, o1, o2, l0, l1, l2, x2d, gn, wa, wb, wo, nf, wr)


def _prefix_tables(n_exp, nch):
    rows = n_exp * nch
    tri = (np.arange(LANES)[:, None] <= np.arange(LANES)[None, :]).astype(np.float32)
    ri = np.arange(rows)
    same = (ri[:, None] // nch) == (ri[None, :] // nch)
    btri = np.logical_and(same, (ri[None, :] % nch) < (ri[:, None] % nch)).astype(np.float32)
    return jnp.asarray(tri, BF16), jnp.asarray(btri, BF16)


def _route_kernel(aff_ref, tri_ref, btri_ref, cin_ref, start_ref, *, cap):
    x = aff_ref[0]
    n_exp, nch, _ = x.shape
    rows = n_exp * nch
    bits = lax.bitcast_convert_type(x, jnp.int32)

    def body(it, ans):
        cand = jnp.bitwise_or(ans, lax.shift_left(jnp.int32(1), 30 - it))
        cnt = jnp.sum(jnp.where(bits >= cand, 1.0, 0.0), axis=(1, 2), keepdims=True)
        return jnp.where(cnt >= cap, cand, ans)

    thr = lax.fori_loop(0, 31, body, jnp.zeros((n_exp, 1, 1), jnp.int32))
    gt = jnp.where(bits > thr, 1.0, 0.0)
    eq = jnp.where(bits == thr, 1.0, 0.0)
    need = cap - jnp.sum(gt, axis=(1, 2), keepdims=True)
    need_rows = jnp.broadcast_to(need, x.shape).reshape(rows, LANES)
    gt2 = gt.reshape(rows, LANES)
    eq2 = eq.reshape(rows, LANES)

    def prefix(mask2):
        cin = jnp.dot(mask2.astype(BF16), tri_ref[...], preferred_element_type=F32)
        tot = jnp.broadcast_to(cin[:, LANES - 1:LANES], cin.shape).astype(BF16)
        off = jnp.dot(btri_ref[...], tot, preferred_element_type=F32)
        return cin, off

    cin_eq, off_eq = prefix(eq2)
    eq_before = off_eq + cin_eq - eq2
    sel = jnp.maximum(gt2, jnp.where(eq_before < need_rows, eq2, 0.0))
    cin, off = prefix(sel)
    cin_ref[0] = cin
    start_ref[0] = off


def _route(afft, cap):
    b, n_exp, t = afft.shape
    nch = t // LANES
    rows = n_exp * nch
    tri, btri = _prefix_tables(n_exp, nch)
    aff4 = afft.reshape(b, n_exp, nch, LANES)
    kern = functools.partial(_route_kernel, cap=cap)
    return pl.pallas_call(
        kern,
        grid=(b,),
        in_specs=[
            pl.BlockSpec((1, n_exp, nch, LANES), lambda i: (i, 0, 0, 0)),
            pl.BlockSpec((LANES, LANES), lambda i: (0, 0)),
            pl.BlockSpec((rows, rows), lambda i: (0, 0)),
        ],
        out_specs=[pl.BlockSpec((1, rows, LANES), lambda i: (i, 0, 0))] * 2,
        out_shape=[jax.ShapeDtypeStruct((b, rows, LANES), F32)] * 2,
        compiler_params=_cparams(("parallel",)),
        name="route",
    )(aff4, tri, btri)


def _pad_rows(a, rows):
    if a.shape[0] == rows:
        return a
    return jnp.concatenate([a, jnp.zeros((rows - a.shape[0], a.shape[1]), a.dtype)], axis=0)


def _compact_kernel(cin_ref, start_ref, aff_ref, idxc_ref, gatec_ref, idxr_ref, *, cap):
    r = _pad_rows(cin_ref[0, 0], LANES)
    st = _pad_rows(start_ref[0, 0], LANES)
    en = st + r[:, LANES - 1:LANES]
    st_row = st.T[0:1, :]
    en_row = en.T[0:1, :]
    sig = lax.broadcasted_iota(jnp.int32, (cap, LANES), 0).astype(F32)
    lane = lax.broadcasted_iota(jnp.int32, (cap, LANES), 1).astype(F32)
    onehot = jnp.where(jnp.logical_and(st_row <= sig, sig < en_row), 1.0, 0.0)
    st_sel = jnp.sum(onehot * st_row, axis=1, keepdims=True)
    c_sel = jnp.sum(onehot * lane, axis=1, keepdims=True)
    oh16 = onehot.astype(BF16)
    r_sel = jnp.dot(oh16, r.astype(BF16), preferred_element_type=F32)
    before = sig[:, 0:1] - st_sel
    pos = jnp.sum(jnp.where(r_sel <= before, 1.0, 0.0), axis=1, keepdims=True)
    idx = c_sel * LANES + pos
    idxc_ref[0] = idx.astype(jnp.int32)
    a = _pad_rows(aff_ref[0, 0], LANES)
    a1 = a.astype(BF16)
    rem = a - a1.astype(F32)
    a2 = rem.astype(BF16)
    a3 = (rem - a2.astype(F32)).astype(BF16)
    a_sel = (jnp.dot(oh16, a1, preferred_element_type=F32) + jnp.dot(oh16, a2, preferred_element_type=F32)
             + jnp.dot(oh16, a3, preferred_element_type=F32))
    gatec_ref[0] = jnp.sum(jnp.where(lane == pos, a_sel, 0.0), axis=1, keepdims=True)
    idx_b = jnp.broadcast_to(idx, (cap, LANES))
    for kb in range(cap // LANES):
        blk = idx_b[kb * LANES:(kb + 1) * LANES, :].T
        idxr_ref[0, :, kb * LANES:(kb + 1) * LANES] = blk[0:1, :].astype(jnp.int32)


def _compact(cin, start, afft, cap):
    b, n_exp, t = afft.shape
    nch = t // LANES
    cin4 = cin.reshape(b, n_exp, nch, LANES)
    start4 = start.reshape(b, n_exp, nch, LANES)
    aff4 = afft.reshape(b, n_exp, nch, LANES)
    in_spec = pl.BlockSpec((1, 1, nch, LANES), lambda bi, e: (bi, e, 0, 0))
    kern = functools.partial(_compact_kernel, cap=cap)
    return pl.pallas_call(
        kern,
        grid=(b, n_exp),
        in_specs=[in_spec, in_spec, in_spec],
        out_specs=[
            pl.BlockSpec((1, cap, 1), lambda bi, e: (bi * n_exp + e, 0, 0)),
            pl.BlockSpec((1, cap, 1), lambda bi, e: (bi * n_exp + e, 0, 0)),
            pl.BlockSpec((1, 1, cap), lambda bi, e: (bi * n_exp + e, 0, 0)),
        ],
        out_shape=[
            jax.ShapeDtypeStruct((b * n_exp, cap, 1), jnp.int32),
            jax.ShapeDtypeStruct((b * n_exp, cap, 1), F32),
            jax.ShapeDtypeStruct((b * n_exp, 1, cap), jnp.int32),
        ],
        compiler_params=_cparams(("parallel", "parallel")),
        name="compact",
    )(cin4, start4, aff4)


MOE_TS = 256


def _ffn_kernel(lo_ref, hi_ref, h_ref, idx_ref, gate_ref, wg_ref, wu_ref, wd_ref, y_ref, xe_ref, *, cap, ts):
    bi, e, ct = pl.program_id(0), pl.program_id(1), pl.program_id(2)
    n_exp = pl.num_programs(1)
    tm = xe_ref.shape[0]
    nsub = tm // LANES
    base = (bi * n_exp + e) * (cap // LANES) + ct * nsub
    lane = lax.broadcasted_iota(jnp.int32, (LANES, ts), 1)
    for c in range(nsub):
        rows = slice(c * LANES, (c + 1) * LANES)
        idxc = idx_ref[0, rows, :]
        xe_ref[rows, :] = jnp.zeros((LANES, D_MODEL), F32)

        def body(s, carry, rows=rows, idxc=idxc):
            t0 = pl.multiple_of(s * ts, ts)
            onehot = jnp.where(idxc - t0 == lane, 1.0, 0.0).astype(BF16)
            xe_ref[rows, :] += jnp.dot(onehot, h_ref[0, pl.ds(t0, ts), :], preferred_element_type=F32)
            return carry

        lax.fori_loop(lo_ref[base + c], hi_ref[base + c] + 1, body, 0)
    xe = xe_ref[...].astype(BF16)
    a = jnp.dot(xe, wg_ref[0], preferred_element_type=F32)
    u = jnp.dot(xe, wu_ref[0], preferred_element_type=F32)
    act = (a * (1.0 / (1.0 + jnp.exp(-a))) * u).astype(BF16)
    y = jnp.dot(act, wd_ref[0], preferred_element_type=F32)
    y_ref[0] = (y * gate_ref[0]).astype(BF16)


def _ffn(h2, idx_col, gate_col, w_gate, w_up, w_down, b, t, cap):
    n_exp = w_gate.shape[0]
    ts = min(MOE_TS, t)
    tm = min(512, cap)
    nchunk = cap // LANES
    first = idx_col[:, 0::LANES, 0] // ts
    last = idx_col[:, LANES - 1::LANES, 0] // ts
    lo_tab = first.reshape(-1).astype(jnp.int32)
    hi_tab = last.reshape(-1).astype(jnp.int32)
    h3 = h2.reshape(b, t, D_MODEL)
    kern = functools.partial(_ffn_kernel, cap=cap, ts=ts)
    wspec = pl.BlockSpec((1, D_MODEL, EXPERT_FF), lambda bi, e, ct, lo, hi: (e, 0, 0))
    grid_spec = pltpu.PrefetchScalarGridSpec(
        num_scalar_prefetch=2,
        grid=(b, n_exp, cap // tm),
        in_specs=[
            pl.BlockSpec((1, t, D_MODEL), lambda bi, e, ct, lo, hi: (bi, 0, 0)),
            pl.BlockSpec((1, tm, 1), lambda bi, e, ct, lo, hi: (bi * n_exp + e, ct, 0)),
            pl.BlockSpec((1, tm, 1), lambda bi, e, ct, lo, hi: (bi * n_exp + e, ct, 0)),
            wspec, wspec,
            pl.BlockSpec((1, EXPERT_FF, D_MODEL), lambda bi, e, ct, lo, hi: (e, 0, 0)),
        ],
        out_specs=pl.BlockSpec((1, tm, D_MODEL), lambda bi, e, ct, lo, hi: (bi * n_exp + e, ct, 0)),
        scratch_shapes=[pltpu.VMEM((tm, D_MODEL), F32)],
    )
    return pl.pallas_call(
        kern,
        grid_spec=grid_spec,
        out_shape=jax.ShapeDtypeStruct((b * n_exp, cap, D_MODEL), BF16),
        compiler_params=_cparams(("arbitrary", "arbitrary", "arbitrary")),
        name="expert_ffn",
    )(lo_tab, hi_tab, h3, idx_col, gate_col, w_gate, w_up, w_down)


def _combine_kernel(klo_ref, khi_ref, x1_ref, y_ref, idxr_ref, nfin_ref, o_ref, acc_ref, *, cap, final_norm):
    bi, ti = pl.program_id(0), pl.program_id(1)
    nts = pl.num_programs(1)
    ts2 = acc_ref.shape[0]
    n_exp = idxr_ref.shape[1]
    acc_ref[...] = jnp.zeros_like(acc_ref)
    tok = ti * ts2 + lax.broadcasted_iota(jnp.int32, (ts2, LANES), 0)
    for e in range(n_exp):
        tab = (bi * nts + ti) * n_exp + e

        def body(k, carry, e=e):
            s0 = pl.multiple_of(k * LANES, LANES)
            yk = y_ref[0, pl.ds(e * cap + s0, LANES), :]
            ir = idxr_ref[0, e:e + 1, pl.ds(s0, LANES)]
            onehot = jnp.where(tok == ir, 1.0, 0.0).astype(BF16)
            acc_ref[...] += jnp.dot(onehot, yk, preferred_element_type=F32)
            return carry

        lax.fori_loop(klo_ref[tab], khi_ref[tab] + 1, body, 0)
    out = x1_ref[...] + acc_ref[...]
    if final_norm:
        out = out * lax.rsqrt(jnp.mean(out * out, axis=-1, keepdims=True) + RMS_EPS) * nfin_ref[...]
    o_ref[...] = out


def _combine(x1, y, idx_row, start, norm_final, b, t, cap, final_norm):
    n_exp = N_EXPERTS
    ts2 = min(MOE_TS, t)
    nts = t // ts2
    nch = t // LANES
    st = start[:, :, 0].reshape(b, n_exp, nch)[:, :, ::ts2 // LANES].astype(jnp.int32)
    en = jnp.concatenate([st[:, :, 1:], jnp.full((b, n_exp, 1), cap, jnp.int32)], axis=2)
    klo = st // LANES
    khi = jnp.where(en > st, (en - 1) // LANES, klo - 1)
    klo_tab = jnp.transpose(klo, (0, 2, 1)).reshape(-1)
    khi_tab = jnp.transpose(khi, (0, 2, 1)).reshape(-1)
    y3 = y.reshape(b, n_exp * cap, D_MODEL)
    idxr3 = idx_row.reshape(b, n_exp, cap)
    kern = functools.partial(_combine_kernel, cap=cap, final_norm=final_norm)
    grid_spec = pltpu.PrefetchScalarGridSpec(
        num_scalar_prefetch=2,
        grid=(b, nts),
        in_specs=[
            pl.BlockSpec((ts2, D_MODEL), lambda bi, ti, lo, hi: (bi * nts + ti, 0)),
            pl.BlockSpec((1, n_exp * cap, D_MODEL), lambda bi, ti, lo, hi: (bi, 0, 0), pipeline_mode=pl.Buffered(1)),
            pl.BlockSpec((1, n_exp, cap), lambda bi, ti, lo, hi: (bi, 0, 0)),
            pl.BlockSpec((1, D_MODEL), lambda bi, ti, lo, hi: (0, 0)),
        ],
        out_specs=pl.BlockSpec((ts2, D_MODEL), lambda bi, ti, lo, hi: (bi * nts + ti, 0)),
        scratch_shapes=[pltpu.VMEM((ts2, D_MODEL), F32)],
    )
    return pl.pallas_call(
        kern,
        grid_spec=grid_spec,
        out_shape=jax.ShapeDtypeStruct((b * t, D_MODEL), F32),
        compiler_params=_cparams(("arbitrary", "arbitrary")),
        name="moe_combine",
    )(klo_tab, khi_tab, x1, y3, idxr3, norm_final.reshape(1, D_MODEL))


GDN_CHUNK = 64


def kernel(x, norm_mix, w_in, conv_w, a_log, dt_bias, gdn_norm, w_branch_a, w_branch_b, w_out, norm_ffn, w_router, w_expert_gate, w_expert_up, w_expert_down, norm_final):
    b, t, d = x.shape
    depth = w_in.shape[0]
    cap = EC_CAPACITY_FACTOR * t // N_EXPERTS
    rope_cos, rope_sin = _rope_tables(t)
    xc = x.reshape(b * t, d)
    for layer in range(depth):
        w_all = _proj_weights(w_in[layer])
        proj = _proj(xc, norm_mix[layer].reshape(1, d), w_all, rope_cos, rope_sin, t)
        proj3 = proj.reshape(b, t, PROJ_W)
        qkv, aux, gcr = _gdn_prep(proj3, conv_w[layer], a_log[layer], dt_bias[layer], GDN_CHUNK)
        o_f, o_b = _gdn_scan(qkv, aux, gcr, GDN_CHUNK)
        att = []
        for gi, (_, dil) in enumerate(ATT_GROUPS):
            o_g, lse_g = _attn_group(proj3, gi, dil)
            att.append((o_g.reshape(b * t, ATT_GW), lse_g.reshape(b * t, ATT_GW)))
        x1, h2, afft = _merge(o_f.reshape(b * t, GDN_W), o_b.reshape(b * t, GDN_W), proj, att, xc,
                              gdn_norm[layer], w_branch_a[layer].astype(BF16), w_branch_b[layer].astype(BF16),
                              w_out[layer].astype(BF16), norm_ffn[layer], w_router[layer], t)
        cin, start = _route(afft, cap)
        idx_col, gate_col, idx_row = _compact(cin, start, afft, cap)
        y = _ffn(h2, idx_col, gate_col, w_expert_gate[layer].astype(BF16), w_expert_up[layer].astype(BF16),
                 w_expert_down[layer].astype(BF16), b, t, cap)
        xc = _combine(x1, y, idx_row, start, norm_final, b, t, cap, final_norm=(layer == depth - 1))
    return xc.reshape(b, t, d)
```

```python
import functools

import numpy as np
import jax
import jax.numpy as jnp
from jax import lax
from jax.experimental import pallas as pl
from jax.experimental.pallas import tpu as pltpu

F32 = jnp.float32
BF16 = jnp.bfloat16

D_MODEL = 1024
GDN_HEADS = 4
GDN_DK = 128
GDN_DV = 128
CONV_WIDTH = 5
ATT_GROUPS = ((128, 1), (512, 4), (2048, 16))
ATT_HPG = 4
ATT_HEAD_DIM = 64
ATT_SPAN = 64
ROPE_THETA = 500000.0
ROPE_DIMS = ATT_HEAD_DIM // 4
N_EXPERTS = 16
EXPERT_FF = 1024
EC_CAPACITY_FACTOR = 2
RMS_EPS = 1e-6
NEG_BIG = -1e30

GDN_W = GDN_HEADS * GDN_DK
GDN_CONV_CH = 3 * GDN_W
ATT_GW = ATT_HPG * ATT_HEAD_DIM
ATT_W = len(ATT_GROUPS) * ATT_GW

C_QKVA = 0
C_Z = C_QKVA + GDN_CONV_CH
C_GATE = C_Z + GDN_W
C_BD = C_GATE + 2 * D_MODEL
PROJ_W = 4608
PROJ_TN = 512

LANES = 128
VMEM_LIMIT = 56 * 1024 * 1024


def _cparams(sem):
    return pltpu.CompilerParams(dimension_semantics=sem, vmem_limit_bytes=VMEM_LIMIT)


def _proj_kernel(x_ref, nw_ref, w_ref, o_ref, h_ref):
    @pl.when(pl.program_id(1) == 0)
    def _():
        x = x_ref[...]
        ms = jnp.mean(x * x, axis=-1, keepdims=True)
        h_ref[...] = (x * lax.rsqrt(ms + RMS_EPS) * nw_ref[...]).astype(BF16)

    o_ref[...] = jnp.dot(h_ref[...], w_ref[...], preferred_element_type=F32)


def _proj(x2d, norm_w, w_all, seq):
    n = x2d.shape[0]
    tm = min(1024, seq)
    return pl.pallas_call(
        _proj_kernel,
        grid=(n // tm, PROJ_W // PROJ_TN),
        in_specs=[
            pl.BlockSpec((tm, D_MODEL), lambda i, j: (i, 0)),
            pl.BlockSpec((1, D_MODEL), lambda i, j: (0, 0)),
            pl.BlockSpec((D_MODEL, PROJ_TN), lambda i, j: (0, j)),
        ],
        out_specs=pl.BlockSpec((tm, PROJ_TN), lambda i, j: (i, j)),
        out_shape=jax.ShapeDtypeStruct((n, PROJ_W), F32),
        scratch_shapes=[pltpu.VMEM((tm, D_MODEL), BF16)],
        compiler_params=_cparams(("parallel", "arbitrary")),
        name="proj",
    )(x2d, norm_w, w_all)


def _att_proj_kernel(x_ref, nw_ref, w_ref, rc_ref, rs_ref, o0_ref, o1_ref, o2_ref, scr_ref):
    x = x_ref[...]
    tm = x.shape[0]
    ms = jnp.mean(x * x, axis=-1, keepdims=True)
    h = (x * lax.rsqrt(ms + RMS_EPS) * nw_ref[...]).astype(BF16)
    reps = ATT_GW // LANES
    cos = jnp.concatenate([rc_ref[...]] * reps, axis=1)
    sin = jnp.concatenate([rs_ref[...]] * reps, axis=1)
    lane = lax.broadcasted_iota(jnp.int32, (tm, ATT_GW), 1) % ATT_HEAD_DIM
    half = ROPE_DIMS // 2
    outs = (o0_ref, o1_ref, o2_ref)
    for g, (_, r) in enumerate(ATT_GROUPS):
        for which in range(3):
            c0 = which * ATT_W + g * ATT_GW
            acc = jnp.dot(h, w_ref[:, c0:c0 + ATT_GW], preferred_element_type=F32)
            if which < 2:
                partner = jnp.where(lane < half, pltpu.roll(acc, ATT_GW - half, 1), pltpu.roll(acc, half, 1))
                acc = acc * cos + partner * sin
            cols = slice(which * ATT_GW, (which + 1) * ATT_GW)
            if r == 1:
                outs[g][0, 0, :, cols] = acc.astype(BF16)
            else:
                for hf in range(reps):
                    scr_ref[hf] = acc[:, hf * LANES:(hf + 1) * LANES]
                for s in range(r):
                    sub = [scr_ref[hf, pl.ds(s, tm // r, stride=r), :] for hf in range(reps)]
                    outs[g][0, s, :, cols] = jnp.concatenate(sub, axis=1).astype(BF16)


def _att_proj(x2d, norm_w, w_att, rope_cos, rope_sin, b, seq):
    n = x2d.shape[0]
    tm = min(1024, seq)
    tpb = seq // tm
    out_specs, out_shape = [], []
    for _, r in ATT_GROUPS:
        out_specs.append(pl.BlockSpec((1, r, tm // r, 3 * ATT_GW), lambda i: (i // tpb, 0, i % tpb, 0)))
        out_shape.append(jax.ShapeDtypeStruct((b, r, seq // r, 3 * ATT_GW), BF16))
    return pl.pallas_call(
        _att_proj_kernel,
        grid=(n // tm,),
        in_specs=[
            pl.BlockSpec((tm, D_MODEL), lambda i: (i, 0)),
            pl.BlockSpec((1, D_MODEL), lambda i: (0, 0)),
            pl.BlockSpec((D_MODEL, 3 * ATT_W), lambda i: (0, 0)),
            pl.BlockSpec((tm, LANES), lambda i: (i % tpb, 0)),
            pl.BlockSpec((tm, LANES), lambda i: (i % tpb, 0)),
        ],
        out_specs=out_specs,
        out_shape=out_shape,
        scratch_shapes=[pltpu.VMEM((ATT_GW // LANES, tm, LANES), F32)],
        compiler_params=_cparams(("parallel",)),
        name="att_proj",
    )(x2d, norm_w, w_att, rope_cos, rope_sin)


def _rope_tables(seq):
    half = ROPE_DIMS // 2
    inv_freq = jnp.power(ROPE_THETA, -jnp.arange(half, dtype=F32) * 2.0 / ROPE_DIMS)
    ang = jnp.arange(seq, dtype=F32)[:, None] * inv_freq[None, :]
    cos, sin = jnp.cos(ang), jnp.sin(ang)
    rest = ATT_HEAD_DIM - ROPE_DIMS
    cos_h = jnp.concatenate([cos, cos, jnp.ones((seq, rest), F32)], axis=1)
    sin_h = jnp.concatenate([-sin, sin, jnp.zeros((seq, rest), F32)], axis=1)
    reps = LANES // ATT_HEAD_DIM
    return jnp.tile(cos_h, (1, reps)), jnp.tile(sin_h, (1, reps))


def _proj_weights(w_in):
    off_z = GDN_CONV_CH
    off_beta = off_z + GDN_W
    off_att = off_beta + 4 * GDN_HEADS
    off_gate = off_att + 3 * ATT_W
    qkva = w_in[:, :off_z]
    z = w_in[:, off_z:off_beta]
    bd = w_in[:, off_beta:off_att]
    att = w_in[:, off_att:off_gate]
    gates = w_in[:, off_gate:]
    att = jnp.concatenate([att[:, :ATT_W] * (ATT_HEAD_DIM ** -0.5), att[:, ATT_W:]], axis=1)
    pad = jnp.zeros((w_in.shape[0], PROJ_W - C_BD - bd.shape[1]), w_in.dtype)
    return jnp.concatenate([qkva, z, gates, bd, pad], axis=1).astype(BF16), att.astype(BF16)


AUX_BETA = 0
AUX_GC = 8
AUX_EG = 16
AUX_EK = 24
AUX_GE = 32
HALO = 8


def _gdn_prep_kernel(xp_ref, xc_ref, xn_ref, bd_ref, cw_ref, alog_ref, dtb_ref,
                     qkv_ref, aux_ref, gcr_ref, xf_ref, *, chunk):
    i = pl.program_id(1)
    nblk = pl.num_programs(1)
    tp = xc_ref.shape[1]
    prev = jnp.where(i > 0, xp_ref[0], 0.0)
    nxt = jnp.where(i < nblk - 1, xn_ref[0], 0.0)
    xf_ref[0:HALO, :] = prev
    xf_ref[HALO:HALO + tp, :] = xc_ref[0]
    xf_ref[HALO + tp:, :] = nxt
    pad = (CONV_WIDTH - 1) // 2
    conv = None
    for j in range(CONV_WIDTH):
        term = xf_ref[pl.ds(HALO - pad + j, tp), :] * cw_ref[j:j + 1, :]
        conv = term if conv is None else conv + term
    u = conv * (1.0 / (1.0 + jnp.exp(-conv)))
    for h in range(2 * GDN_HEADS):
        sl = slice(h * GDN_DK, (h + 1) * GDN_DK)
        uh = u[:, sl]
        scale = lax.rsqrt(jnp.sum(uh * uh, axis=-1, keepdims=True) + 1e-6)
        if h < GDN_HEADS:
            scale = scale * (GDN_DK ** -0.5)
        qkv_ref[0, :, sl] = uh * scale
    qkv_ref[0, :, 2 * GDN_W:] = u[:, 2 * GDN_W:]

    nh2 = 2 * GDN_HEADS
    bd = bd_ref[0]
    lane = lax.broadcasted_iota(jnp.int32, bd.shape, 1)
    row = lax.broadcasted_iota(jnp.int32, bd.shape, 0) % chunk
    beta = 1.0 / (1.0 + jnp.exp(-bd))
    sp_in = bd + dtb_ref[...]
    softplus = jnp.maximum(sp_in, 0.0) + jnp.log(1.0 + jnp.exp(-jnp.abs(sp_in)))
    g = -jnp.exp(alog_ref[...]) * softplus
    g = jnp.where(jnp.logical_and(lane >= nh2, lane < 2 * nh2), g, 0.0)
    pre = g
    suf = g
    s = 1
    while s < chunk:
        pre = pre + jnp.where(row >= s, pltpu.roll(pre, s, 0), 0.0)
        suf = suf + jnp.where(row < chunk - s, pltpu.roll(suf, tp - s, 0), 0.0)
        s *= 2
    total = pre + suf - g
    is_fwd = lane < nh2 + GDN_HEADS
    gc = jnp.where(is_fwd, pre, suf)
    eg = jnp.exp(gc)
    ek = jnp.exp(total - gc)
    ge = jnp.exp(total)
    aux = jnp.where(lane < nh2, beta, 0.0)
    aux = aux + gc
    aux = aux + jnp.where(jnp.logical_and(lane >= AUX_EG, lane < AUX_EG + nh2), pltpu.roll(eg, AUX_EG - AUX_GC, 1), 0.0)
    aux = aux + jnp.where(jnp.logical_and(lane >= AUX_EK, lane < AUX_EK + nh2), pltpu.roll(ek, AUX_EK - AUX_GC, 1), 0.0)
    aux = aux + jnp.where(jnp.logical_and(lane >= AUX_GE, lane < AUX_GE + nh2), pltpu.roll(ge, AUX_GE - AUX_GC, 1), 0.0)
    aux_ref[0] = aux
    gct = gc.T
    for c in range(tp // chunk):
        gcr_ref[0, c] = gct[AUX_GC:AUX_GC + nh2, c * chunk:(c + 1) * chunk]


def _gdn_prep(proj3, conv_w, a_log, dt_bias, chunk):
    b, t, _ = proj3.shape
    tp = min(512, t)
    nblk = t // tp
    nh2 = 2 * GDN_HEADS
    alog_row = jnp.zeros((1, LANES), F32).at[0, nh2:2 * nh2].set(a_log.reshape(-1))
    dtb_row = jnp.zeros((1, LANES), F32).at[0, nh2:2 * nh2].set(dt_bias.reshape(-1))
    hb = tp // HALO
    kern = functools.partial(_gdn_prep_kernel, chunk=chunk)
    return pl.pallas_call(
        kern,
        grid=(b, nblk),
        in_specs=[
            pl.BlockSpec((1, HALO, GDN_CONV_CH), lambda bi, i: (bi, jnp.maximum(i * hb - 1, 0), 0)),
            pl.BlockSpec((1, tp, GDN_CONV_CH), lambda bi, i: (bi, i, 0)),
            pl.BlockSpec((1, HALO, GDN_CONV_CH), lambda bi, i: (bi, jnp.minimum((i + 1) * hb, t // HALO - 1), 0)),
            pl.BlockSpec((1, tp, LANES), lambda bi, i: (bi, i, C_BD // LANES)),
            pl.BlockSpec((CONV_WIDTH, GDN_CONV_CH), lambda bi, i: (0, 0)),
            pl.BlockSpec((1, LANES), lambda bi, i: (0, 0)),
            pl.BlockSpec((1, LANES), lambda bi, i: (0, 0)),
        ],
        out_specs=[
            pl.BlockSpec((1, tp, GDN_CONV_CH), lambda bi, i: (bi, i, 0)),
            pl.BlockSpec((1, tp, LANES), lambda bi, i: (bi, i, 0)),
            pl.BlockSpec((1, tp // chunk, nh2, chunk), lambda bi, i: (bi, i, 0, 0)),
        ],
        out_shape=[
            jax.ShapeDtypeStruct((b, t, GDN_CONV_CH), F32),
            jax.ShapeDtypeStruct((b, t, LANES), F32),
            jax.ShapeDtypeStruct((b, t // chunk, nh2, chunk), F32),
        ],
        scratch_shapes=[pltpu.VMEM((tp + 2 * HALO, GDN_CONV_CH), F32)],
        compiler_params=_cparams(("parallel", "parallel")),
        name="gdn_prep",
    )(proj3, proj3, proj3, proj3, conv_w, alog_row, dtb_row)


_DN_T = (((1,), (1,)), ((), ()))
_DN_L = (((0,), (0,)), ((), ()))


def _gdn_chunk_step(units, chunk):
    ri = lax.broadcasted_iota(jnp.int32, (chunk, chunk), 0)
    ci = lax.broadcasted_iota(jnp.int32, (chunk, chunk), 1)
    eye = jnp.where(ri == ci, 1.0, 0.0)
    for un in units:
        col, aux = un["col"], un["aux"]
        beta = aux[:, AUX_BETA + col:AUX_BETA + col + 1]
        gc = aux[:, AUX_GC + col:AUX_GC + col + 1]
        eg = aux[:, AUX_EG + col:AUX_EG + col + 1]
        ek = aux[:, AUX_EK + col:AUX_EK + col + 1]
        un["ge"] = aux[0:1, AUX_GE + col:AUX_GE + col + 1]
        if un["d"] == 0:
            incl, strict = ri >= ci, ri > ci
        else:
            incl, strict = ri <= ci, ri < ci
        un["strict"] = strict
        un["decay"] = jnp.where(incl, jnp.exp(jnp.minimum(gc - un["gcr"], 0.0)), 0.0)
        kb = un["k"] * beta
        un["kbq"] = jnp.concatenate([kb, un["q"]], axis=0).astype(BF16)
        un["kf"] = un["k"].astype(BF16)
        un["rhs"] = jnp.concatenate([un["v"] * beta, kb * eg], axis=1).astype(BF16)
        un["qd"] = un["q"] * eg
        un["kd"] = (un["k"] * ek).astype(BF16)
    for un in units:
        un["kkqk"] = lax.dot_general(un["kbq"], un["kf"], _DN_T, preferred_element_type=F32)
    for un in units:
        lmat = jnp.where(un["strict"], un["kkqk"][:chunk] * un["decay"], 0.0)
        un["amat"] = (un["kkqk"][chunk:] * un["decay"]).astype(BF16)
        un["pw"] = lmat
        un["tinv"] = eye - lmat
    n_sq = max(chunk.bit_length() - 2, 0)
    for i in range(n_sq + 1):
        for un in units:
            pw16 = un["pw"].astype(BF16)
            if i == 0:
                un["prod"] = jnp.dot(pw16, pw16, preferred_element_type=F32)
            elif i < n_sq:
                lhs = jnp.concatenate([un["pw"], un["tinv"]], axis=0).astype(BF16)
                un["prod"] = jnp.dot(lhs, pw16, preferred_element_type=F32)
            else:
                un["prod"] = jnp.dot(un["tinv"].astype(BF16), pw16, preferred_element_type=F32)
        for un in units:
            if i == 0:
                un["pw"] = un["prod"]
            elif i < n_sq:
                un["tinv"] = un["tinv"] + un["prod"][chunk:]
                un["pw"] = un["prod"][:chunk]
            else:
                un["tinv"] = un["tinv"] + un["prod"]
    for un in units:
        un["uw"] = jnp.dot(un["tinv"].astype(BF16), un["rhs"], preferred_element_type=F32)
    for un in units:
        lhs = jnp.concatenate([un["uw"][:, GDN_DV:], un["qd"]], axis=0).astype(BF16)
        un["ws"] = jnp.dot(lhs, un["s"].astype(BF16), preferred_element_type=F32)
    for un in units:
        un["v_new"] = (un["uw"][:, :GDN_DV] - un["ws"][:chunk]).astype(BF16)
    for un in units:
        un["o"] = un["ws"][chunk:] + jnp.dot(un["amat"], un["v_new"], preferred_element_type=F32)
        un["s_new"] = un["s"] * un["ge"] + lax.dot_general(un["kd"], un["v_new"], _DN_L, preferred_element_type=F32)


def _gdn_scan_kernel(qkvf_ref, qkvb_ref, auxf_ref, auxb_ref, gcrf_ref, gcrb_ref,
                     of_ref, ob_ref, s_ref, *, chunk):
    @pl.when(pl.program_id(1) == 0)
    def _():
        s_ref[...] = jnp.zeros_like(s_ref)

    nc = qkvf_ref.shape[1] // chunk
    dirs = ((qkvf_ref, auxf_ref, gcrf_ref, of_ref), (qkvb_ref, auxb_ref, gcrb_ref, ob_ref))

    def body(c, carry):
        units = []
        for d, (qkv_ref, aux_ref, gcr_ref, o_ref) in enumerate(dirs):
            cc = c if d == 0 else nc - 1 - c
            r0 = pl.multiple_of(cc * chunk, chunk)
            aux = aux_ref[0, pl.ds(r0, chunk), :]
            gcr_all = gcr_ref[0, cc]
            for h in range(GDN_HEADS):
                si = d * GDN_HEADS + h
                units.append(dict(
                    q=qkv_ref[0, pl.ds(r0, chunk), h * GDN_DK:(h + 1) * GDN_DK],
                    k=qkv_ref[0, pl.ds(r0, chunk), GDN_W + h * GDN_DK:GDN_W + (h + 1) * GDN_DK],
                    v=qkv_ref[0, pl.ds(r0, chunk), 2 * GDN_W + h * GDN_DV:2 * GDN_W + (h + 1) * GDN_DV],
                    aux=aux, gcr=gcr_all[si:si + 1, :], s=s_ref[si], col=si, d=d, h=h, r0=r0, o_ref=o_ref))
        _gdn_chunk_step(units, chunk)
        for un in units:
            s_ref[un["col"]] = un["s_new"]
            un["o_ref"][0, pl.ds(un["r0"], chunk), un["h"] * GDN_DV:(un["h"] + 1) * GDN_DV] = un["o"]
        return carry

    lax.fori_loop(0, nc, body, 0)


def _gdn_scan(qkv, aux, gcr, chunk):
    b, t, _ = qkv.shape
    tb = min(512, t)
    nb = t // tb
    nh2 = 2 * GDN_HEADS
    kern = functools.partial(_gdn_scan_kernel, chunk=chunk)
    fwd = lambda bi, i: (bi, i, 0)
    bwd = lambda bi, i: (bi, nb - 1 - i, 0)
    return pl.pallas_call(
        kern,
        grid=(b, nb),
        in_specs=[
            pl.BlockSpec((1, tb, GDN_CONV_CH), fwd),
            pl.BlockSpec((1, tb, GDN_CONV_CH), bwd),
            pl.BlockSpec((1, tb, LANES), fwd),
            pl.BlockSpec((1, tb, LANES), bwd),
            pl.BlockSpec((1, tb // chunk, nh2, chunk), lambda bi, i: (bi, i, 0, 0)),
            pl.BlockSpec((1, tb // chunk, nh2, chunk), lambda bi, i: (bi, nb - 1 - i, 0, 0)),
        ],
        out_specs=[pl.BlockSpec((1, tb, GDN_W), fwd), pl.BlockSpec((1, tb, GDN_W), bwd)],
        out_shape=[jax.ShapeDtypeStruct((b, t, GDN_W), F32)] * 2,
        scratch_shapes=[pltpu.VMEM((nh2, GDN_DK, GDN_DV), F32)],
        compiler_params=_cparams(("parallel", "arbitrary")),
        name="gdn_scan",
    )(qkv, qkv, aux, aux, gcr, gcr)


ATT_QT = 128


ATT_BLK = 2048


def _attn_tile(q, kt, vt, valid):
    head_of_lane = lax.broadcasted_iota(jnp.int32, (ATT_QT, ATT_GW), 1) // ATT_HEAD_DIM
    o_acc = jnp.zeros((ATT_QT, ATT_GW), F32)
    lse_acc = jnp.zeros((ATT_QT, ATT_GW), F32)
    for h in range(ATT_HPG):
        in_head = head_of_lane == h
        qh = jnp.where(in_head, q, jnp.zeros_like(q))
        s = lax.dot_general(qh, kt, _DN_T, preferred_element_type=F32)
        s = jnp.where(valid, s, NEG_BIG)
        mx = jnp.max(s, axis=-1, keepdims=True)
        p = jnp.exp(s - mx)
        l = jnp.sum(p, axis=-1, keepdims=True)
        pv = jnp.dot(p.astype(BF16), vt, preferred_element_type=F32)
        o_acc = jnp.where(in_head, pv / l, o_acc)
        lse_acc = jnp.where(in_head, mx + jnp.log(l), lse_acc)
    return o_acc, lse_acc


def _attn_kernel(*refs, seq):
    n_g = len(ATT_GROUPS)
    ins = refs[:7 * n_g]
    y_ref = refs[7 * n_g]
    kv_scr = refs[7 * n_g + 1:7 * n_g + 1 + 2 * n_g]
    o_scr, l_scr = refs[7 * n_g + 1 + 2 * n_g:]
    i = pl.program_id(1)
    n_half = ATT_GW // LANES
    nk = ATT_QT + 2 * ATT_SPAN
    qi = lax.broadcasted_iota(jnp.int32, (ATT_QT, nk), 0)
    ki = lax.broadcasted_iota(jnp.int32, (ATT_QT, nk), 1)
    rel = ki - qi
    band = jnp.logical_and(rel >= 0, rel <= 2 * ATT_SPAN)
    for g, (_, r) in enumerate(ATT_GROUPS):
        q_ref, kp_ref, kc_ref, kn_ref, vp_ref, vc_ref, vn_ref = ins[7 * g:7 * g + 7]
        kf_ref, vf_ref = kv_scr[2 * g:2 * g + 2]
        rows = ATT_BLK // r
        m_total = seq // r
        for full_ref, parts in ((kf_ref, (kp_ref, kc_ref, kn_ref)), (vf_ref, (vp_ref, vc_ref, vn_ref))):
            full_ref[:, 0:ATT_SPAN, :] = parts[0][0]
            full_ref[:, ATT_SPAN:ATT_SPAN + rows, :] = parts[1][0]
            full_ref[:, ATT_SPAN + rows:, :] = parts[2][0]
        tiles = rows // ATT_QT

        def tile(idx, carry, r=r, g=g, rows=rows, tiles=tiles, m_total=m_total,
                 q_ref=q_ref, kf_ref=kf_ref, vf_ref=vf_ref):
            s = idx // tiles
            j = idx % tiles
            r0 = pl.multiple_of(j * ATT_QT, ATT_QT)
            q = q_ref[0, s, pl.ds(r0, ATT_QT), :]
            kt = kf_ref[s, pl.ds(r0, nk), :]
            vt = vf_ref[s, pl.ds(r0, nk), :]
            mk = i * rows + r0 - ATT_SPAN + ki
            valid = jnp.logical_and(band, jnp.logical_and(mk >= 0, mk < m_total))
            o, lse = _attn_tile(q, kt, vt, valid)
            for hf in range(n_half):
                lanes = slice(hf * LANES, (hf + 1) * LANES)
                if r == 1:
                    dst = pl.ds(r0, ATT_QT)
                else:
                    dst = pl.ds(r0 * r + s, ATT_QT, stride=r)
                o_scr[g * n_half + hf, dst, :] = o[:, lanes]
                l_scr[g * n_half + hf, dst, :] = lse[:, lanes]
            return carry

        lax.fori_loop(0, r * tiles, tile, 0)

    cb = 256

    def comb(c, carry):
        r0 = pl.multiple_of(c * cb, cb)
        halves = []
        for hf in range(n_half):
            ls = [l_scr[g * n_half + hf, pl.ds(r0, cb), :] for g in range(n_g)]
            lm = functools.reduce(jnp.maximum, ls)
            es = [jnp.exp(lv - lm) for lv in ls]
            num = sum(es[g] * o_scr[g * n_half + hf, pl.ds(r0, cb), :] for g in range(n_g))
            halves.append(num / sum(es))
        y_ref[pl.ds(r0, cb), :] = jnp.concatenate(halves, axis=1).astype(BF16)
        return carry

    lax.fori_loop(0, ATT_BLK // cb, comb, 0)


def _attention(att_qkv, b, seq):
    assert seq % ATT_BLK == 0
    nblk = seq // ATT_BLK
    in_specs, scratch, args = [], [], []
    for (_, r), arr in zip(ATT_GROUPS, att_qkv):
        rows = ATT_BLK // r
        assert rows % ATT_QT == 0
        hb = rows // ATT_SPAN
        last = seq // r // ATT_SPAN - 1

        def center(c, r=r, rows=rows):
            return pl.BlockSpec((1, r, rows, ATT_GW), lambda bi, i: (bi, 0, i, c))

        def prev(c, r=r, hb=hb):
            return pl.BlockSpec((1, r, ATT_SPAN, ATT_GW), lambda bi, i: (bi, 0, jnp.maximum(i * hb - 1, 0), c))

        def nxt(c, r=r, hb=hb, last=last):
            return pl.BlockSpec((1, r, ATT_SPAN, ATT_GW), lambda bi, i: (bi, 0, jnp.minimum((i + 1) * hb, last), c))

        in_specs += [center(0), prev(1), center(1), nxt(1), prev(2), center(2), nxt(2)]
        args += [arr] * 7
        scratch += [pltpu.VMEM((r, rows + 2 * ATT_SPAN, ATT_GW), BF16)] * 2
    n_g = len(ATT_GROUPS)
    scratch += [pltpu.VMEM((n_g * (ATT_GW // LANES), ATT_BLK, LANES), F32)] * 2
    kern = functools.partial(_attn_kernel, seq=seq)
    return pl.pallas_call(
        kern,
        grid=(b, nblk),
        in_specs=in_specs,
        out_specs=pl.BlockSpec((ATT_BLK, ATT_GW), lambda bi, i: (bi * nblk + i, 0)),
        out_shape=jax.ShapeDtypeStruct((b * seq, ATT_GW), BF16),
        scratch_shapes=scratch,
        compiler_params=_cparams(("parallel", "parallel")),
        name="attention",
    )(*args)


def _merge_kernel(of_ref, ob_ref, z_ref, ga_ref, gb_ref, yb_ref,
                  x_ref, gn_ref, wa_ref, wb_ref, wo_ref, nf_ref, wr_ref,
                  x1_ref, h2_ref, afft_ref):
    o = of_ref[...] + ob_ref[...]
    z = z_ref[...]
    ya_parts = []
    for h in range(GDN_HEADS):
        sl = slice(h * GDN_DV, (h + 1) * GDN_DV)
        oh = o[:, sl]
        oh = oh * lax.rsqrt(jnp.mean(oh * oh, axis=-1, keepdims=True) + RMS_EPS) * gn_ref[...]
        zh = z[:, sl]
        ya_parts.append(oh * (zh * (1.0 / (1.0 + jnp.exp(-zh)))))
    y_a = jnp.concatenate(ya_parts, axis=1).astype(BF16)
    y_b = yb_ref[...]
    ga = 1.0 / (1.0 + jnp.exp(-ga_ref[...]))
    gb = 1.0 / (1.0 + jnp.exp(-gb_ref[...]))
    mixed = (ga * jnp.dot(y_a, wa_ref[...], preferred_element_type=F32)
             + gb * jnp.dot(y_b, wb_ref[...], preferred_element_type=F32))
    x1 = x_ref[...] + jnp.dot(mixed.astype(BF16), wo_ref[...], preferred_element_type=F32)
    x1_ref[...] = x1
    h2 = (x1 * lax.rsqrt(jnp.mean(x1 * x1, axis=-1, keepdims=True) + RMS_EPS) * nf_ref[...]).astype(BF16)
    h2_ref[...] = h2
    logits = jnp.dot(h2, wr_ref[...], preferred_element_type=F32)
    lane = lax.broadcasted_iota(jnp.int32, logits.shape, 1)
    logits = jnp.where(lane < N_EXPERTS, logits, NEG_BIG)
    lmax = jnp.max(logits, axis=-1, keepdims=True)
    ex = jnp.exp(logits - lmax)
    aff = ex / jnp.sum(ex, axis=-1, keepdims=True)
    afft_ref[0] = aff.T[:N_EXPERTS, :]


def _merge(of, ob, proj, y_b, x2d, gdn_norm, wa, wb, wo, norm_ffn, w_router, seq):
    n = x2d.shape[0]
    tm = min(512, seq)
    tpb = seq // tm
    wr = jnp.zeros((D_MODEL, LANES), BF16).at[:, :N_EXPERTS].set(w_router.astype(BF16))
    row = lambda w: pl.BlockSpec((tm, w), lambda i: (i, 0))
    full = lambda a: pl.BlockSpec(a.shape, lambda i: (0,) * a.ndim)
    gn = gdn_norm.reshape(1, GDN_DV)
    nf = norm_ffn.reshape(1, D_MODEL)
    in_specs = [
        row(GDN_W), row(GDN_W),
        pl.BlockSpec((tm, GDN_W), lambda i: (i, C_Z // GDN_W)),
        pl.BlockSpec((tm, D_MODEL), lambda i: (i, C_GATE // D_MODEL)),
        pl.BlockSpec((tm, D_MODEL), lambda i: (i, C_GATE // D_MODEL + 1)),
        row(ATT_GW),
        row(D_MODEL), full(gn), full(wa), full(wb), full(wo), full(nf), full(wr),
    ]
    return pl.pallas_call(
        _merge_kernel,
        grid=(n // tm,),
        in_specs=in_specs,
        out_specs=[
            row(D_MODEL), row(D_MODEL),
            pl.BlockSpec((1, N_EXPERTS, tm), lambda i: (i // tpb, 0, i % tpb)),
        ],
        out_shape=[
            jax.ShapeDtypeStruct((n, D_MODEL), F32),
            jax.ShapeDtypeStruct((n, D_MODEL), BF16),
            jax.ShapeDtypeStruct((n // seq, N_EXPERTS, seq), F32),
        ],
        compiler_params=_cparams(("parallel",)),
        name="merge",
    )(of, ob, proj, proj, proj, y_b, x2d, gn, wa, wb, wo, nf, wr)


def _prefix_tables(n_exp, nch):
    rows = n_exp * nch
    tri = (np.arange(LANES)[:, None] <= np.arange(LANES)[None, :]).astype(np.float32)
    ri = np.arange(rows)
    same = (ri[:, None] // nch) == (ri[None, :] // nch)
    btri = np.logical_and(same, (ri[None, :] % nch) < (ri[:, None] % nch)).astype(np.float32)
    return jnp.asarray(tri, BF16), jnp.asarray(btri, BF16)


def _route_kernel(aff_ref, tri_ref, btri_ref, cin_ref, start_ref, *, cap):
    x = aff_ref[0]
    n_exp, nch, _ = x.shape
    rows = n_exp * nch
    bits = lax.bitcast_convert_type(x, jnp.int32)

    def body(it, ans):
        cand = jnp.bitwise_or(ans, lax.shift_left(jnp.int32(1), 30 - it))
        cnt = jnp.sum(jnp.where(bits >= cand, 1.0, 0.0), axis=(1, 2), keepdims=True)
        return jnp.where(cnt >= cap, cand, ans)

    thr = lax.fori_loop(0, 31, body, jnp.zeros((n_exp, 1, 1), jnp.int32))
    gt = jnp.where(bits > thr, 1.0, 0.0)
    eq = jnp.where(bits == thr, 1.0, 0.0)
    need = cap - jnp.sum(gt, axis=(1, 2), keepdims=True)
    need_rows = jnp.broadcast_to(need, x.shape).reshape(rows, LANES)
    gt2 = gt.reshape(rows, LANES)
    eq2 = eq.reshape(rows, LANES)

    def prefix(mask2):
        cin = jnp.dot(mask2.astype(BF16), tri_ref[...], preferred_element_type=F32)
        tot = jnp.broadcast_to(cin[:, LANES - 1:LANES], cin.shape).astype(BF16)
        off = jnp.dot(btri_ref[...], tot, preferred_element_type=F32)
        return cin, off

    cin_eq, off_eq = prefix(eq2)
    eq_before = off_eq + cin_eq - eq2
    sel = jnp.maximum(gt2, jnp.where(eq_before < need_rows, eq2, 0.0))
    cin, off = prefix(sel)
    cin_ref[0] = cin
    start_ref[0] = off


def _route(afft, cap):
    b, n_exp, t = afft.shape
    nch = t // LANES
    rows = n_exp * nch
    tri, btri = _prefix_tables(n_exp, nch)
    aff4 = afft.reshape(b, n_exp, nch, LANES)
    kern = functools.partial(_route_kernel, cap=cap)
    return pl.pallas_call(
        kern,
        grid=(b,),
        in_specs=[
            pl.BlockSpec((1, n_exp, nch, LANES), lambda i: (i, 0, 0, 0)),
            pl.BlockSpec((LANES, LANES), lambda i: (0, 0)),
            pl.BlockSpec((rows, rows), lambda i: (0, 0)),
        ],
        out_specs=[pl.BlockSpec((1, rows, LANES), lambda i: (i, 0, 0))] * 2,
        out_shape=[jax.ShapeDtypeStruct((b, rows, LANES), F32)] * 2,
        compiler_params=_cparams(("parallel",)),
        name="route",
    )(aff4, tri, btri)


def _pad_rows(a, rows):
    if a.shape[0] == rows:
        return a
    return jnp.concatenate([a, jnp.zeros((rows - a.shape[0], a.shape[1]), a.dtype)], axis=0)


def _compact_kernel(cin_ref, start_ref, aff_ref, idxc_ref, gatec_ref, idxr_ref, *, cap):
    r = _pad_rows(cin_ref[0, 0], LANES)
    st = _pad_rows(start_ref[0, 0], LANES)
    en = st + r[:, LANES - 1:LANES]
    st_row = st.T[0:1, :]
    en_row = en.T[0:1, :]
    sig = lax.broadcasted_iota(jnp.int32, (cap, LANES), 0).astype(F32)
    lane = lax.broadcasted_iota(jnp.int32, (cap, LANES), 1).astype(F32)
    onehot = jnp.where(jnp.logical_and(st_row <= sig, sig < en_row), 1.0, 0.0)
    st_sel = jnp.sum(onehot * st_row, axis=1, keepdims=True)
    c_sel = jnp.sum(onehot * lane, axis=1, keepdims=True)
    oh16 = onehot.astype(BF16)
    r_sel = jnp.dot(oh16, r.astype(BF16), preferred_element_type=F32)
    before = sig[:, 0:1] - st_sel
    pos = jnp.sum(jnp.where(r_sel <= before, 1.0, 0.0), axis=1, keepdims=True)
    idx = c_sel * LANES + pos
    idxc_ref[0] = idx.astype(jnp.int32)
    a = _pad_rows(aff_ref[0, 0], LANES)
    a1 = a.astype(BF16)
    rem = a - a1.astype(F32)
    a2 = rem.astype(BF16)
    a3 = (rem - a2.astype(F32)).astype(BF16)
    a_sel = (jnp.dot(oh16, a1, preferred_element_type=F32) + jnp.dot(oh16, a2, preferred_element_type=F32)
             + jnp.dot(oh16, a3, preferred_element_type=F32))
    gatec_ref[0] = jnp.sum(jnp.where(lane == pos, a_sel, 0.0), axis=1, keepdims=True)
    idx_b = jnp.broadcast_to(idx, (cap, LANES))
    for kb in range(cap // LANES):
        blk = idx_b[kb * LANES:(kb + 1) * LANES, :].T
        idxr_ref[0, :, kb * LANES:(kb + 1) * LANES] = blk[0:1, :].astype(jnp.int32)


def _compact(cin, start, afft, cap):
    b, n_exp, t = afft.shape
    nch = t // LANES
    cin4 = cin.reshape(b, n_exp, nch, LANES)
    start4 = start.reshape(b, n_exp, nch, LANES)
    aff4 = afft.reshape(b, n_exp, nch, LANES)
    in_spec = pl.BlockSpec((1, 1, nch, LANES), lambda bi, e: (bi, e, 0, 0))
    kern = functools.partial(_compact_kernel, cap=cap)
    return pl.pallas_call(
        kern,
        grid=(b, n_exp),
        in_specs=[in_spec, in_spec, in_spec],
        out_specs=[
            pl.BlockSpec((1, cap, 1), lambda bi, e: (bi * n_exp + e, 0, 0)),
            pl.BlockSpec((1, cap, 1), lambda bi, e: (bi * n_exp + e, 0, 0)),
            pl.BlockSpec((1, 1, cap), lambda bi, e: (bi * n_exp + e, 0, 0)),
        ],
        out_shape=[
            jax.ShapeDtypeStruct((b * n_exp, cap, 1), jnp.int32),
            jax.ShapeDtypeStruct((b * n_exp, cap, 1), F32),
            jax.ShapeDtypeStruct((b * n_exp, 1, cap), jnp.int32),
        ],
        compiler_params=_cparams(("parallel", "parallel")),
        name="compact",
    )(cin4, start4, aff4)


MOE_TS = 256


def _ffn_kernel(lo_ref, hi_ref, h_ref, idx_ref, gate_ref, wg_ref, wu_ref, wd_ref, y_ref, xe_ref, *, cap, ts):
    bi, e, ct = pl.program_id(0), pl.program_id(1), pl.program_id(2)
    n_exp = pl.num_programs(1)
    tm = xe_ref.shape[0]
    nsub = tm // LANES
    base = (bi * n_exp + e) * (cap // LANES) + ct * nsub
    lane = lax.broadcasted_iota(jnp.int32, (LANES, ts), 1)
    for c in range(nsub):
        rows = slice(c * LANES, (c + 1) * LANES)
        idxc = idx_ref[0, rows, :]
        xe_ref[rows, :] = jnp.zeros((LANES, D_MODEL), F32)

        def body(s, carry, rows=rows, idxc=idxc):
            t0 = pl.multiple_of(s * ts, ts)
            onehot = jnp.where(idxc - t0 == lane, 1.0, 0.0).astype(BF16)
            xe_ref[rows, :] += jnp.dot(onehot, h_ref[0, pl.ds(t0, ts), :], preferred_element_type=F32)
            return carry

        lax.fori_loop(lo_ref[base + c], hi_ref[base + c] + 1, body, 0)
    xe = xe_ref[...].astype(BF16)
    a = jnp.dot(xe, wg_ref[0], preferred_element_type=F32)
    u = jnp.dot(xe, wu_ref[0], preferred_element_type=F32)
    act = (a * (1.0 / (1.0 + jnp.exp(-a))) * u).astype(BF16)
    y = jnp.dot(act, wd_ref[0], preferred_element_type=F32)
    y_ref[0] = (y * gate_ref[0]).astype(BF16)


def _ffn(h2, idx_col, gate_col, w_gate, w_up, w_down, b, t, cap):
    n_exp = w_gate.shape[0]
    ts = min(MOE_TS, t)
    tm = min(512, cap)
    nchunk = cap // LANES
    first = idx_col[:, 0::LANES, 0] // ts
    last = idx_col[:, LANES - 1::LANES, 0] // ts
    lo_tab = first.reshape(-1).astype(jnp.int32)
    hi_tab = last.reshape(-1).astype(jnp.int32)
    h3 = h2.reshape(b, t, D_MODEL)
    kern = functools.partial(_ffn_kernel, cap=cap, ts=ts)
    wspec = pl.BlockSpec((1, D_MODEL, EXPERT_FF), lambda bi, e, ct, lo, hi: (e, 0, 0))
    grid_spec = pltpu.PrefetchScalarGridSpec(
        num_scalar_prefetch=2,
        grid=(b, n_exp, cap // tm),
        in_specs=[
            pl.BlockSpec((1, t, D_MODEL), lambda bi, e, ct, lo, hi: (bi, 0, 0)),
            pl.BlockSpec((1, tm, 1), lambda bi, e, ct, lo, hi: (bi * n_exp + e, ct, 0)),
            pl.BlockSpec((1, tm, 1), lambda bi, e, ct, lo, hi: (bi * n_exp + e, ct, 0)),
            wspec, wspec,
            pl.BlockSpec((1, EXPERT_FF, D_MODEL), lambda bi, e, ct, lo, hi: (e, 0, 0)),
        ],
        out_specs=pl.BlockSpec((1, tm, D_MODEL), lambda bi, e, ct, lo, hi: (bi * n_exp + e, ct, 0)),
        scratch_shapes=[pltpu.VMEM((tm, D_MODEL), F32)],
    )
    return pl.pallas_call(
        kern,
        grid_spec=grid_spec,
        out_shape=jax.ShapeDtypeStruct((b * n_exp, cap, D_MODEL), BF16),
        compiler_params=_cparams(("arbitrary", "arbitrary", "arbitrary")),
        name="expert_ffn",
    )(lo_tab, hi_tab, h3, idx_col, gate_col, w_gate, w_up, w_down)


def _combine_kernel(klo_ref, khi_ref, x1_ref, y_ref, idxr_ref, nfin_ref, o_ref, acc_ref, *, cap, final_norm):
    bi, ti = pl.program_id(0), pl.program_id(1)
    nts = pl.num_programs(1)
    ts2 = acc_ref.shape[0]
    n_exp = idxr_ref.shape[1]
    acc_ref[...] = jnp.zeros_like(acc_ref)
    tok = ti * ts2 + lax.broadcasted_iota(jnp.int32, (ts2, LANES), 0)
    for e in range(n_exp):
        tab = (bi * nts + ti) * n_exp + e

        def body(k, carry, e=e):
            s0 = pl.multiple_of(k * LANES, LANES)
            yk = y_ref[0, pl.ds(e * cap + s0, LANES), :]
            ir = idxr_ref[0, e:e + 1, pl.ds(s0, LANES)]
            onehot = jnp.where(tok == ir, 1.0, 0.0).astype(BF16)
            acc_ref[...] += jnp.dot(onehot, yk, preferred_element_type=F32)
            return carry

        lax.fori_loop(klo_ref[tab], khi_ref[tab] + 1, body, 0)
    out = x1_ref[...] + acc_ref[...]
    if final_norm:
        out = out * lax.rsqrt(jnp.mean(out * out, axis=-1, keepdims=True) + RMS_EPS) * nfin_ref[...]
    o_ref[...] = out


def _combine(x1, y, idx_row, start, norm_final, b, t, cap, final_norm):
    n_exp = N_EXPERTS
    ts2 = min(MOE_TS, t)
    nts = t // ts2
    nch = t // LANES
    st = start[:, :, 0].reshape(b, n_exp, nch)[:, :, ::ts2 // LANES].astype(jnp.int32)
    en = jnp.concatenate([st[:, :, 1:], jnp.full((b, n_exp, 1), cap, jnp.int32)], axis=2)
    klo = st // LANES
    khi = jnp.where(en > st, (en - 1) // LANES, klo - 1)
    klo_tab = jnp.transpose(klo, (0, 2, 1)).reshape(-1)
    khi_tab = jnp.transpose(khi, (0, 2, 1)).reshape(-1)
    y3 = y.reshape(b, n_exp * cap, D_MODEL)
    idxr3 = idx_row.reshape(b, n_exp, cap)
    kern = functools.partial(_combine_kernel, cap=cap, final_norm=final_norm)
    grid_spec = pltpu.PrefetchScalarGridSpec(
        num_scalar_prefetch=2,
        grid=(b, nts),
        in_specs=[
            pl.BlockSpec((ts2, D_MODEL), lambda bi, ti, lo, hi: (bi * nts + ti, 0)),
            pl.BlockSpec((1, n_exp * cap, D_MODEL), lambda bi, ti, lo, hi: (bi, 0, 0), pipeline_mode=pl.Buffered(1)),
            pl.BlockSpec((1, n_exp, cap), lambda bi, ti, lo, hi: (bi, 0, 0)),
            pl.BlockSpec((1, D_MODEL), lambda bi, ti, lo, hi: (0, 0)),
        ],
        out_specs=pl.BlockSpec((ts2, D_MODEL), lambda bi, ti, lo, hi: (bi * nts + ti, 0)),
        scratch_shapes=[pltpu.VMEM((ts2, D_MODEL), F32)],
    )
    return pl.pallas_call(
        kern,
        grid_spec=grid_spec,
        out_shape=jax.ShapeDtypeStruct((b * t, D_MODEL), F32),
        compiler_params=_cparams(("arbitrary", "arbitrary")),
        name="moe_combine",
    )(klo_tab, khi_tab, x1, y3, idxr3, norm_final.reshape(1, D_MODEL))


GDN_CHUNK = 64


def kernel(x, norm_mix, w_in, conv_w, a_log, dt_bias, gdn_norm, w_branch_a, w_branch_b, w_out, norm_ffn, w_router, w_expert_gate, w_expert_up, w_expert_down, norm_final):
    b, t, d = x.shape
    depth = w_in.shape[0]
    cap = EC_CAPACITY_FACTOR * t // N_EXPERTS
    rope_cos, rope_sin = _rope_tables(t)
    xc = x.reshape(b * t, d)
    for layer in range(depth):
        w_all, w_att = _proj_weights(w_in[layer])
        nw = norm_mix[layer].reshape(1, d)
        proj = _proj(xc, nw, w_all, t)
        att_qkv = _att_proj(xc, nw, w_att, rope_cos, rope_sin, b, t)
        proj3 = proj.reshape(b, t, PROJ_W)
        qkv, aux, gcr = _gdn_prep(proj3, conv_w[layer], a_log[layer], dt_bias[layer], GDN_CHUNK)
        o_f, o_b = _gdn_scan(qkv, aux, gcr, GDN_CHUNK)
        y_b = _attention(att_qkv, b, t)
        x1, h2, afft = _merge(o_f.reshape(b * t, GDN_W), o_b.reshape(b * t, GDN_W), proj, y_b, xc,
                              gdn_norm[layer], w_branch_a[layer].astype(BF16), w_branch_b[layer].astype(BF16),
                              w_out[layer].astype(BF16), norm_ffn[layer], w_router[layer], t)
        cin, start = _route(afft, cap)
        idx_col, gate_col, idx_row = _compact(cin, start, afft, cap)
        y = _ffn(h2, idx_col, gate_col, w_expert_gate[layer].astype(BF16), w_expert_up[layer].astype(BF16),
                 w_expert_down[layer].astype(BF16), b, t, cap)
        xc = _combine(x1, y, idx_row, start, norm_final, b, t, cap, final_norm=(layer == depth - 1))
    return xc.reshape(b, t, d)
```

```python
import functools

import numpy as np
import jax
import jax.numpy as jnp
from jax import lax
from jax.experimental import pallas as pl
from jax.experimental.pallas import tpu as pltpu

F32 = jnp.float32
BF16 = jnp.bfloat16

D_MODEL = 1024
GDN_HEADS = 4
GDN_DK = 128
GDN_DV = 128
CONV_WIDTH = 5
ATT_GROUPS = ((128, 1), (512, 4), (2048, 16))
ATT_HPG = 4
ATT_HEAD_DIM = 64
ATT_SPAN = 64
ROPE_THETA = 500000.0
ROPE_DIMS = ATT_HEAD_DIM // 4
N_EXPERTS = 16
EXPERT_FF = 1024
EC_CAPACITY_FACTOR = 2
RMS_EPS = 1e-6
NEG_BIG = -1e30

GDN_W = GDN_HEADS * GDN_DK
GDN_CONV_CH = 3 * GDN_W
ATT_GW = ATT_HPG * ATT_HEAD_DIM
ATT_W = len(ATT_GROUPS) * ATT_GW

C_QKVA = 0
C_Z = C_QKVA + GDN_CONV_CH
C_GATE = C_Z + GDN_W
PROJ_W = C_GATE + 2 * D_MODEL
PROJ_TN = 512
BD_W = 4 * GDN_HEADS

LANES = 128
VMEM_LIMIT = 56 * 1024 * 1024


def _cparams(sem):
    return pltpu.CompilerParams(dimension_semantics=sem, vmem_limit_bytes=VMEM_LIMIT)


def _proj_kernel(x_ref, nw_ref, w_ref, wbd_ref, o_ref, bd_ref):
    x = x_ref[...]
    ms = jnp.mean(x * x, axis=-1, keepdims=True)
    h = (x * lax.rsqrt(ms + RMS_EPS) * nw_ref[...]).astype(BF16)
    for j in range(PROJ_W // PROJ_TN):
        cols = slice(j * PROJ_TN, (j + 1) * PROJ_TN)
        o_ref[:, cols] = jnp.dot(h, w_ref[:, cols], preferred_element_type=F32).astype(BF16)
    bd_ref[...] = jnp.dot(h, wbd_ref[...], preferred_element_type=F32)


def _proj(x2d, norm_w, w_all, w_bd):
    n = x2d.shape[0]
    tm = min(512, n)
    return pl.pallas_call(
        _proj_kernel,
        grid=(n // tm,),
        in_specs=[
            pl.BlockSpec((tm, D_MODEL), lambda i: (i, 0)),
            pl.BlockSpec((1, D_MODEL), lambda i: (0, 0)),
            pl.BlockSpec((D_MODEL, PROJ_W), lambda i: (0, 0), pipeline_mode=pl.Buffered(1)),
            pl.BlockSpec((D_MODEL, LANES), lambda i: (0, 0)),
        ],
        out_specs=[pl.BlockSpec((tm, PROJ_W), lambda i: (i, 0)), pl.BlockSpec((tm, LANES), lambda i: (i, 0))],
        out_shape=[jax.ShapeDtypeStruct((n, PROJ_W), BF16), jax.ShapeDtypeStruct((n, LANES), F32)],
        compiler_params=_cparams(("parallel",)),
        name="proj",
    )(x2d, norm_w, w_all, w_bd)


def _att_proj_kernel(x_ref, nw_ref, w_ref, rc_ref, rs_ref, o0_ref, o1_ref, o2_ref, scr_ref):
    x = x_ref[...]
    tm = x.shape[0]
    ms = jnp.mean(x * x, axis=-1, keepdims=True)
    h = (x * lax.rsqrt(ms + RMS_EPS) * nw_ref[...]).astype(BF16)
    reps = ATT_GW // LANES
    cos = jnp.concatenate([rc_ref[...]] * reps, axis=1)
    sin = jnp.concatenate([rs_ref[...]] * reps, axis=1)
    lane = lax.broadcasted_iota(jnp.int32, (tm, ATT_GW), 1) % ATT_HEAD_DIM
    half = ROPE_DIMS // 2
    outs = (o0_ref, o1_ref, o2_ref)
    for g, (_, r) in enumerate(ATT_GROUPS):
        for which in range(3):
            c0 = which * ATT_W + g * ATT_GW
            acc = jnp.dot(h, w_ref[:, c0:c0 + ATT_GW], preferred_element_type=F32)
            if which < 2:
                partner = jnp.where(lane < half, pltpu.roll(acc, ATT_GW - half, 1), pltpu.roll(acc, half, 1))
                acc = acc * cos + partner * sin
            cols = slice(which * ATT_GW, (which + 1) * ATT_GW)
            if r == 1:
                outs[g][0, 0, :, cols] = acc.astype(BF16)
            else:
                for hf in range(reps):
                    scr_ref[hf] = acc[:, hf * LANES:(hf + 1) * LANES]
                for s in range(r):
                    sub = [scr_ref[hf, pl.ds(s, tm // r, stride=r), :] for hf in range(reps)]
                    outs[g][0, s, :, cols] = jnp.concatenate(sub, axis=1).astype(BF16)


def _att_proj(x2d, norm_w, w_att, rope_cos, rope_sin, b, seq):
    n = x2d.shape[0]
    tm = min(1024, seq)
    tpb = seq // tm
    out_specs, out_shape = [], []
    for _, r in ATT_GROUPS:
        out_specs.append(pl.BlockSpec((1, r, tm // r, 3 * ATT_GW), lambda i: (i // tpb, 0, i % tpb, 0)))
        out_shape.append(jax.ShapeDtypeStruct((b, r, seq // r, 3 * ATT_GW), BF16))
    return pl.pallas_call(
        _att_proj_kernel,
        grid=(n // tm,),
        in_specs=[
            pl.BlockSpec((tm, D_MODEL), lambda i: (i, 0)),
            pl.BlockSpec((1, D_MODEL), lambda i: (0, 0)),
            pl.BlockSpec((D_MODEL, 3 * ATT_W), lambda i: (0, 0)),
            pl.BlockSpec((tm, LANES), lambda i: (i % tpb, 0)),
            pl.BlockSpec((tm, LANES), lambda i: (i % tpb, 0)),
        ],
        out_specs=out_specs,
        out_shape=out_shape,
        scratch_shapes=[pltpu.VMEM((ATT_GW // LANES, tm, LANES), F32)],
        compiler_params=_cparams(("parallel",)),
        name="att_proj",
    )(x2d, norm_w, w_att, rope_cos, rope_sin)


def _rope_tables(seq):
    half = ROPE_DIMS // 2
    inv_freq = jnp.power(ROPE_THETA, -jnp.arange(half, dtype=F32) * 2.0 / ROPE_DIMS)
    ang = jnp.arange(seq, dtype=F32)[:, None] * inv_freq[None, :]
    cos, sin = jnp.cos(ang), jnp.sin(ang)
    rest = ATT_HEAD_DIM - ROPE_DIMS
    cos_h = jnp.concatenate([cos, cos, jnp.ones((seq, rest), F32)], axis=1)
    sin_h = jnp.concatenate([-sin, sin, jnp.zeros((seq, rest), F32)], axis=1)
    reps = LANES // ATT_HEAD_DIM
    return jnp.tile(cos_h, (1, reps)), jnp.tile(sin_h, (1, reps))


def _proj_weights(w_in):
    off_z = GDN_CONV_CH
    off_beta = off_z + GDN_W
    off_att = off_beta + 4 * GDN_HEADS
    off_gate = off_att + 3 * ATT_W
    qkva = w_in[:, :off_z]
    z = w_in[:, off_z:off_beta]
    bd = w_in[:, off_beta:off_att]
    att = w_in[:, off_att:off_gate]
    gates = w_in[:, off_gate:]
    att = jnp.concatenate([att[:, :ATT_W] * (ATT_HEAD_DIM ** -0.5), att[:, ATT_W:]], axis=1)
    bd = jnp.concatenate([bd, jnp.zeros((w_in.shape[0], LANES - BD_W), w_in.dtype)], axis=1)
    return jnp.concatenate([qkva, z, gates], axis=1).astype(BF16), bd.astype(BF16), att.astype(BF16)


AUX_BETA = 0
AUX_GC = 8
AUX_EG = 16
AUX_EK = 24
AUX_GE = 32
HALO = 16


def _gdn_prep_kernel(xp_ref, xc_ref, xn_ref, bd_ref, cw_ref, alog_ref, dtb_ref,
                     qkv_ref, aux_ref, gcr_ref, xf_ref, *, chunk):
    i = pl.program_id(1)
    nblk = pl.num_programs(1)
    tp = xc_ref.shape[1]
    prev = jnp.where(i > 0, xp_ref[0].astype(F32), 0.0)
    nxt = jnp.where(i < nblk - 1, xn_ref[0].astype(F32), 0.0)
    xf_ref[0:HALO, :] = prev
    xf_ref[HALO:HALO + tp, :] = xc_ref[0].astype(F32)
    xf_ref[HALO + tp:, :] = nxt
    pad = (CONV_WIDTH - 1) // 2
    conv = None
    for j in range(CONV_WIDTH):
        term = xf_ref[pl.ds(HALO - pad + j, tp), :] * cw_ref[j:j + 1, :]
        conv = term if conv is None else conv + term
    u = conv * (1.0 / (1.0 + jnp.exp(-conv)))
    for h in range(2 * GDN_HEADS):
        sl = slice(h * GDN_DK, (h + 1) * GDN_DK)
        uh = u[:, sl]
        scale = lax.rsqrt(jnp.sum(uh * uh, axis=-1, keepdims=True) + 1e-6)
        if h < GDN_HEADS:
            scale = scale * (GDN_DK ** -0.5)
        qkv_ref[0, :, sl] = (uh * scale).astype(BF16)
    qkv_ref[0, :, 2 * GDN_W:] = u[:, 2 * GDN_W:].astype(BF16)

    nh2 = 2 * GDN_HEADS
    bd = bd_ref[0]
    lane = lax.broadcasted_iota(jnp.int32, bd.shape, 1)
    row = lax.broadcasted_iota(jnp.int32, bd.shape, 0) % chunk
    beta = 1.0 / (1.0 + jnp.exp(-bd))
    sp_in = bd + dtb_ref[...]
    softplus = jnp.maximum(sp_in, 0.0) + jnp.log(1.0 + jnp.exp(-jnp.abs(sp_in)))
    g = -jnp.exp(alog_ref[...]) * softplus
    g = jnp.where(jnp.logical_and(lane >= nh2, lane < 2 * nh2), g, 0.0)
    pre = g
    suf = g
    s = 1
    while s < chunk:
        pre = pre + jnp.where(row >= s, pltpu.roll(pre, s, 0), 0.0)
        suf = suf + jnp.where(row < chunk - s, pltpu.roll(suf, tp - s, 0), 0.0)
        s *= 2
    total = pre + suf - g
    is_fwd = lane < nh2 + GDN_HEADS
    gc = jnp.where(is_fwd, pre, suf)
    eg = jnp.exp(gc)
    ek = jnp.exp(total - gc)
    ge = jnp.exp(total)
    aux = jnp.where(lane < nh2, beta, 0.0)
    aux = aux + gc
    aux = aux + jnp.where(jnp.logical_and(lane >= AUX_EG, lane < AUX_EG + nh2), pltpu.roll(eg, AUX_EG - AUX_GC, 1), 0.0)
    aux = aux + jnp.where(jnp.logical_and(lane >= AUX_EK, lane < AUX_EK + nh2), pltpu.roll(ek, AUX_EK - AUX_GC, 1), 0.0)
    aux = aux + jnp.where(jnp.logical_and(lane >= AUX_GE, lane < AUX_GE + nh2), pltpu.roll(ge, AUX_GE - AUX_GC, 1), 0.0)
    aux_ref[0] = aux
    gct = gc.T
    for c in range(tp // chunk):
        gcr_ref[0, c] = gct[AUX_GC:AUX_GC + nh2, c * chunk:(c + 1) * chunk]


def _gdn_prep(proj3, bd3, conv_w, a_log, dt_bias, chunk):
    b, t, _ = proj3.shape
    tp = min(512, t)
    nblk = t // tp
    nh2 = 2 * GDN_HEADS
    alog_row = jnp.zeros((1, LANES), F32).at[0, nh2:2 * nh2].set(a_log.reshape(-1))
    dtb_row = jnp.zeros((1, LANES), F32).at[0, nh2:2 * nh2].set(dt_bias.reshape(-1))
    hb = tp // HALO
    kern = functools.partial(_gdn_prep_kernel, chunk=chunk)
    return pl.pallas_call(
        kern,
        grid=(b, nblk),
        in_specs=[
            pl.BlockSpec((1, HALO, GDN_CONV_CH), lambda bi, i: (bi, jnp.maximum(i * hb - 1, 0), 0)),
            pl.BlockSpec((1, tp, GDN_CONV_CH), lambda bi, i: (bi, i, 0)),
            pl.BlockSpec((1, HALO, GDN_CONV_CH), lambda bi, i: (bi, jnp.minimum((i + 1) * hb, t // HALO - 1), 0)),
            pl.BlockSpec((1, tp, LANES), lambda bi, i: (bi, i, 0)),
            pl.BlockSpec((CONV_WIDTH, GDN_CONV_CH), lambda bi, i: (0, 0)),
            pl.BlockSpec((1, LANES), lambda bi, i: (0, 0)),
            pl.BlockSpec((1, LANES), lambda bi, i: (0, 0)),
        ],
        out_specs=[
            pl.BlockSpec((1, tp, GDN_CONV_CH), lambda bi, i: (bi, i, 0)),
            pl.BlockSpec((1, tp, LANES), lambda bi, i: (bi, i, 0)),
            pl.BlockSpec((1, tp // chunk, nh2, chunk), lambda bi, i: (bi, i, 0, 0)),
        ],
        out_shape=[
            jax.ShapeDtypeStruct((b, t, GDN_CONV_CH), BF16),
            jax.ShapeDtypeStruct((b, t, LANES), F32),
            jax.ShapeDtypeStruct((b, t // chunk, nh2, chunk), F32),
        ],
        scratch_shapes=[pltpu.VMEM((tp + 2 * HALO, GDN_CONV_CH), F32)],
        compiler_params=_cparams(("parallel", "parallel")),
        name="gdn_prep",
    )(proj3, proj3, proj3, bd3, conv_w, alog_row, dtb_row)


_DN_T = (((1,), (1,)), ((), ()))
_DN_L = (((0,), (0,)), ((), ()))


def _gdn_chunk_step(units, chunk):
    ri = lax.broadcasted_iota(jnp.int32, (chunk, chunk), 0)
    ci = lax.broadcasted_iota(jnp.int32, (chunk, chunk), 1)
    eye = jnp.where(ri == ci, 1.0, 0.0)
    for un in units:
        col, aux = un["col"], un["aux"]
        beta = aux[:, AUX_BETA + col:AUX_BETA + col + 1]
        gc = aux[:, AUX_GC + col:AUX_GC + col + 1]
        eg = aux[:, AUX_EG + col:AUX_EG + col + 1]
        ek = aux[:, AUX_EK + col:AUX_EK + col + 1]
        un["ge"] = aux[0:1, AUX_GE + col:AUX_GE + col + 1]
        if un["d"] == 0:
            incl, strict = ri >= ci, ri > ci
        else:
            incl, strict = ri <= ci, ri < ci
        un["strict"] = strict
        un["decay"] = jnp.where(incl, jnp.exp(jnp.minimum(gc - un["gcr"], 0.0)), 0.0)
        kb = un["k"] * beta
        un["kbq"] = jnp.concatenate([kb, un["q"]], axis=0).astype(BF16)
        un["kf"] = un["k"].astype(BF16)
        un["rhs"] = jnp.concatenate([un["v"] * beta, kb * eg], axis=1).astype(BF16)
        un["qd"] = un["q"] * eg
        un["kd"] = (un["k"] * ek).astype(BF16)
    for un in units:
        un["kkqk"] = lax.dot_general(un["kbq"], un["kf"], _DN_T, preferred_element_type=F32)
    for un in units:
        lmat = jnp.where(un["strict"], un["kkqk"][:chunk] * un["decay"], 0.0)
        un["amat"] = (un["kkqk"][chunk:] * un["decay"]).astype(BF16)
        un["pw"] = lmat
        un["tinv"] = eye - lmat
    n_sq = max(chunk.bit_length() - 2, 0)
    for i in range(n_sq + 1):
        for un in units:
            pw16 = un["pw"].astype(BF16)
            if i == 0:
                un["prod"] = jnp.dot(pw16, pw16, preferred_element_type=F32)
            elif i < n_sq:
                lhs = jnp.concatenate([un["pw"], un["tinv"]], axis=0).astype(BF16)
                un["prod"] = jnp.dot(lhs, pw16, preferred_element_type=F32)
            else:
                un["prod"] = jnp.dot(un["tinv"].astype(BF16), pw16, preferred_element_type=F32)
        for un in units:
            if i == 0:
                un["pw"] = un["prod"]
            elif i < n_sq:
                un["tinv"] = un["tinv"] + un["prod"][chunk:]
                un["pw"] = un["prod"][:chunk]
            else:
                un["tinv"] = un["tinv"] + un["prod"]
    for un in units:
        un["uw"] = jnp.dot(un["tinv"].astype(BF16), un["rhs"], preferred_element_type=F32)
    for un in units:
        lhs = jnp.concatenate([un["uw"][:, GDN_DV:], un["qd"]], axis=0).astype(BF16)
        un["ws"] = jnp.dot(lhs, un["s"].astype(BF16), preferred_element_type=F32)
    for un in units:
        un["v_new"] = (un["uw"][:, :GDN_DV] - un["ws"][:chunk]).astype(BF16)
    for un in units:
        un["o"] = un["ws"][chunk:] + jnp.dot(un["amat"], un["v_new"], preferred_element_type=F32)
        un["s_new"] = un["s"] * un["ge"] + lax.dot_general(un["kd"], un["v_new"], _DN_L, preferred_element_type=F32)


def _gdn_scan_kernel(qkvf_ref, qkvb_ref, auxf_ref, auxb_ref, gcrf_ref, gcrb_ref,
                     of_ref, ob_ref, s_ref, *, chunk):
    @pl.when(pl.program_id(1) == 0)
    def _():
        s_ref[...] = jnp.zeros_like(s_ref)

    nc = qkvf_ref.shape[1] // chunk
    dirs = ((qkvf_ref, auxf_ref, gcrf_ref, of_ref), (qkvb_ref, auxb_ref, gcrb_ref, ob_ref))

    def body(c, carry):
        units = []
        for d, (qkv_ref, aux_ref, gcr_ref, o_ref) in enumerate(dirs):
            cc = c if d == 0 else nc - 1 - c
            r0 = pl.multiple_of(cc * chunk, chunk)
            aux = aux_ref[0, pl.ds(r0, chunk), :]
            gcr_all = gcr_ref[0, cc]
            for h in range(GDN_HEADS):
                si = d * GDN_HEADS + h
                units.append(dict(
                    q=qkv_ref[0, pl.ds(r0, chunk), h * GDN_DK:(h + 1) * GDN_DK],
                    k=qkv_ref[0, pl.ds(r0, chunk), GDN_W + h * GDN_DK:GDN_W + (h + 1) * GDN_DK],
                    v=qkv_ref[0, pl.ds(r0, chunk), 2 * GDN_W + h * GDN_DV:2 * GDN_W + (h + 1) * GDN_DV],
                    aux=aux, gcr=gcr_all[si:si + 1, :], s=s_ref[si], col=si, d=d, h=h, r0=r0, o_ref=o_ref))
        _gdn_chunk_step(units, chunk)
        for un in units:
            s_ref[un["col"]] = un["s_new"]
            un["o_ref"][0, pl.ds(un["r0"], chunk), un["h"] * GDN_DV:(un["h"] + 1) * GDN_DV] = un["o"]
        return carry

    lax.fori_loop(0, nc, body, 0)


def _gdn_scan(qkv, aux, gcr, chunk):
    b, t, _ = qkv.shape
    tb = min(512, t)
    nb = t // tb
    nh2 = 2 * GDN_HEADS
    kern = functools.partial(_gdn_scan_kernel, chunk=chunk)
    fwd = lambda bi, i: (bi, i, 0)
    bwd = lambda bi, i: (bi, nb - 1 - i, 0)
    return pl.pallas_call(
        kern,
        grid=(b, nb),
        in_specs=[
            pl.BlockSpec((1, tb, GDN_CONV_CH), fwd),
            pl.BlockSpec((1, tb, GDN_CONV_CH), bwd),
            pl.BlockSpec((1, tb, LANES), fwd),
            pl.BlockSpec((1, tb, LANES), bwd),
            pl.BlockSpec((1, tb // chunk, nh2, chunk), lambda bi, i: (bi, i, 0, 0)),
            pl.BlockSpec((1, tb // chunk, nh2, chunk), lambda bi, i: (bi, nb - 1 - i, 0, 0)),
        ],
        out_specs=[pl.BlockSpec((1, tb, GDN_W), fwd), pl.BlockSpec((1, tb, GDN_W), bwd)],
        out_shape=[jax.ShapeDtypeStruct((b, t, GDN_W), F32)] * 2,
        scratch_shapes=[pltpu.VMEM((nh2, GDN_DK, GDN_DV), F32)],
        compiler_params=_cparams(("parallel", "arbitrary")),
        name="gdn_scan",
    )(qkv, qkv, aux, aux, gcr, gcr)


ATT_QT = 128


ATT_BLK = 2048


ATT_TILES_PER_ITER = 2


def _attn_tiles(tiles):
    head_of_lane = lax.broadcasted_iota(jnp.int32, (ATT_QT, ATT_GW), 1) // ATT_HEAD_DIM
    in_head = [head_of_lane == h for h in range(ATT_HPG)]
    for tl in tiles:
        q = tl["q"]
        q4 = jnp.concatenate([jnp.where(m, q, jnp.zeros_like(q)) for m in in_head], axis=0)
        tl["s"] = lax.dot_general(q4, tl["kt"], _DN_T, preferred_element_type=F32)
    for tl in tiles:
        nk = tl["s"].shape[1]
        s = jnp.where(tl["valid"][None], tl["s"].reshape(ATT_HPG, ATT_QT, nk), NEG_BIG)
        mx = jnp.max(s, axis=-1, keepdims=True)
        p = jnp.exp(s - mx)
        tl["l"] = jnp.sum(p, axis=-1, keepdims=True)
        tl["mx"] = mx
        tl["p"] = p.reshape(ATT_HPG * ATT_QT, nk).astype(BF16)
    for tl in tiles:
        tl["pv"] = jnp.dot(tl["p"], tl["vt"], preferred_element_type=F32)
    for tl in tiles:
        o = jnp.zeros((ATT_QT, ATT_GW), F32)
        lse = jnp.zeros((ATT_QT, ATT_GW), F32)
        for h in range(ATT_HPG):
            rows = slice(h * ATT_QT, (h + 1) * ATT_QT)
            o = jnp.where(in_head[h], tl["pv"][rows] / tl["l"][h], o)
            lse = jnp.where(in_head[h], tl["mx"][h] + jnp.log(tl["l"][h]), lse)
        tl["o"], tl["lse"] = o, lse


def _attn_kernel(*refs, seq):
    n_g = len(ATT_GROUPS)
    ins = refs[:7 * n_g]
    y_ref = refs[7 * n_g]
    kv_scr = refs[7 * n_g + 1:7 * n_g + 1 + 2 * n_g]
    o_scr, l_scr = refs[7 * n_g + 1 + 2 * n_g:]
    i = pl.program_id(1)
    n_half = ATT_GW // LANES
    nk = ATT_QT + 2 * ATT_SPAN
    qi = lax.broadcasted_iota(jnp.int32, (ATT_QT, nk), 0)
    ki = lax.broadcasted_iota(jnp.int32, (ATT_QT, nk), 1)
    rel = ki - qi
    band = jnp.logical_and(rel >= 0, rel <= 2 * ATT_SPAN)
    for g, (_, r) in enumerate(ATT_GROUPS):
        q_ref, kp_ref, kc_ref, kn_ref, vp_ref, vc_ref, vn_ref = ins[7 * g:7 * g + 7]
        kf_ref, vf_ref = kv_scr[2 * g:2 * g + 2]
        rows = ATT_BLK // r
        m_total = seq // r
        for full_ref, parts in ((kf_ref, (kp_ref, kc_ref, kn_ref)), (vf_ref, (vp_ref, vc_ref, vn_ref))):
            full_ref[:, 0:ATT_SPAN, :] = parts[0][0]
            full_ref[:, ATT_SPAN:ATT_SPAN + rows, :] = parts[1][0]
            full_ref[:, ATT_SPAN + rows:, :] = parts[2][0]
        tiles = rows // ATT_QT

        assert (r * tiles) % ATT_TILES_PER_ITER == 0

        def tile_group(it, carry, r=r, g=g, rows=rows, tiles=tiles, m_total=m_total,
                       q_ref=q_ref, kf_ref=kf_ref, vf_ref=vf_ref):
            group = []
            for u in range(ATT_TILES_PER_ITER):
                idx = it * ATT_TILES_PER_ITER + u
                s = idx // tiles
                r0 = pl.multiple_of((idx % tiles) * ATT_QT, ATT_QT)
                mk = i * rows + r0 - ATT_SPAN + ki
                group.append(dict(
                    res=s, r0=r0, q=q_ref[0, s, pl.ds(r0, ATT_QT), :],
                    kt=kf_ref[s, pl.ds(r0, nk), :], vt=vf_ref[s, pl.ds(r0, nk), :],
                    valid=jnp.logical_and(band, jnp.logical_and(mk >= 0, mk < m_total))))
            _attn_tiles(group)
            for tl in group:
                for hf in range(n_half):
                    lanes = slice(hf * LANES, (hf + 1) * LANES)
                    if r == 1:
                        dst = pl.ds(tl["r0"], ATT_QT)
                    else:
                        dst = pl.ds(tl["r0"] * r + tl["res"], ATT_QT, stride=r)
                    o_scr[g * n_half + hf, dst, :] = tl["o"][:, lanes]
                    l_scr[g * n_half + hf, dst, :] = tl["lse"][:, lanes]
            return carry

        lax.fori_loop(0, r * tiles // ATT_TILES_PER_ITER, tile_group, 0)

    cb = 256

    def comb(c, carry):
        r0 = pl.multiple_of(c * cb, cb)
        halves = []
        for hf in range(n_half):
            ls = [l_scr[g * n_half + hf, pl.ds(r0, cb), :] for g in range(n_g)]
            lm = functools.reduce(jnp.maximum, ls)
            es = [jnp.exp(lv - lm) for lv in ls]
            num = sum(es[g] * o_scr[g * n_half + hf, pl.ds(r0, cb), :] for g in range(n_g))
            halves.append(num / sum(es))
        y_ref[pl.ds(r0, cb), :] = jnp.concatenate(halves, axis=1).astype(BF16)
        return carry

    lax.fori_loop(0, ATT_BLK // cb, comb, 0)


def _attention(att_qkv, b, seq):
    assert seq % ATT_BLK == 0
    nblk = seq // ATT_BLK
    in_specs, scratch, args = [], [], []
    for (_, r), arr in zip(ATT_GROUPS, att_qkv):
        rows = ATT_BLK // r
        assert rows % ATT_QT == 0
        hb = rows // ATT_SPAN
        last = seq // r // ATT_SPAN - 1

        def center(c, r=r, rows=rows):
            return pl.BlockSpec((1, r, rows, ATT_GW), lambda bi, i: (bi, 0, i, c))

        def prev(c, r=r, hb=hb):
            return pl.BlockSpec((1, r, ATT_SPAN, ATT_GW), lambda bi, i: (bi, 0, jnp.maximum(i * hb - 1, 0), c))

        def nxt(c, r=r, hb=hb, last=last):
            return pl.BlockSpec((1, r, ATT_SPAN, ATT_GW), lambda bi, i: (bi, 0, jnp.minimum((i + 1) * hb, last), c))

        in_specs += [center(0), prev(1), center(1), nxt(1), prev(2), center(2), nxt(2)]
        args += [arr] * 7
        scratch += [pltpu.VMEM((r, rows + 2 * ATT_SPAN, ATT_GW), BF16)] * 2
    n_g = len(ATT_GROUPS)
    scratch += [pltpu.VMEM((n_g * (ATT_GW // LANES), ATT_BLK, LANES), F32)] * 2
    kern = functools.partial(_attn_kernel, seq=seq)
    return pl.pallas_call(
        kern,
        grid=(b, nblk),
        in_specs=in_specs,
        out_specs=pl.BlockSpec((ATT_BLK, ATT_GW), lambda bi, i: (bi * nblk + i, 0)),
        out_shape=jax.ShapeDtypeStruct((b * seq, ATT_GW), BF16),
        scratch_shapes=scratch,
        compiler_params=_cparams(("parallel", "parallel")),
        name="attention",
    )(*args)


def _merge_kernel(of_ref, ob_ref, z_ref, ga_ref, gb_ref, yb_ref,
                  x_ref, gn_ref, wa_ref, wb_ref, wo_ref, nf_ref, wr_ref,
                  x1_ref, h2t_ref, afft_ref):
    o = of_ref[...] + ob_ref[...]
    z = z_ref[...].astype(F32)
    ya_parts = []
    for h in range(GDN_HEADS):
        sl = slice(h * GDN_DV, (h + 1) * GDN_DV)
        oh = o[:, sl]
        oh = oh * lax.rsqrt(jnp.mean(oh * oh, axis=-1, keepdims=True) + RMS_EPS) * gn_ref[...]
        zh = z[:, sl]
        ya_parts.append(oh * (zh * (1.0 / (1.0 + jnp.exp(-zh)))))
    y_a = jnp.concatenate(ya_parts, axis=1).astype(BF16)
    y_b = yb_ref[...]
    ga = 1.0 / (1.0 + jnp.exp(-ga_ref[...].astype(F32)))
    gb = 1.0 / (1.0 + jnp.exp(-gb_ref[...].astype(F32)))
    mixed = (ga * jnp.dot(y_a, wa_ref[...], preferred_element_type=F32)
             + gb * jnp.dot(y_b, wb_ref[...], preferred_element_type=F32))
    x1 = x_ref[...] + jnp.dot(mixed.astype(BF16), wo_ref[...], preferred_element_type=F32)
    x1_ref[...] = x1
    h2f = x1 * lax.rsqrt(jnp.mean(x1 * x1, axis=-1, keepdims=True) + RMS_EPS) * nf_ref[...]
    h2 = h2f.astype(BF16)
    h2t_ref[0] = h2f.T.astype(BF16)
    logits = jnp.dot(h2, wr_ref[...], preferred_element_type=F32)
    lane = lax.broadcasted_iota(jnp.int32, logits.shape, 1)
    logits = jnp.where(lane < N_EXPERTS, logits, NEG_BIG)
    lmax = jnp.max(logits, axis=-1, keepdims=True)
    ex = jnp.exp(logits - lmax)
    aff = ex / jnp.sum(ex, axis=-1, keepdims=True)
    afft_ref[0] = aff.T[:N_EXPERTS, :]


def _merge(of, ob, proj, y_b, x2d, gdn_norm, wa, wb, wo, norm_ffn, w_router, seq):
    n = x2d.shape[0]
    tm = min(512, seq)
    tpb = seq // tm
    wr = jnp.zeros((D_MODEL, LANES), BF16).at[:, :N_EXPERTS].set(w_router.astype(BF16))
    row = lambda w: pl.BlockSpec((tm, w), lambda i: (i, 0))
    full = lambda a: pl.BlockSpec(a.shape, lambda i: (0,) * a.ndim)
    gn = gdn_norm.reshape(1, GDN_DV)
    nf = norm_ffn.reshape(1, D_MODEL)
    in_specs = [
        row(GDN_W), row(GDN_W),
        pl.BlockSpec((tm, GDN_W), lambda i: (i, C_Z // GDN_W)),
        pl.BlockSpec((tm, D_MODEL), lambda i: (i, C_GATE // D_MODEL)),
        pl.BlockSpec((tm, D_MODEL), lambda i: (i, C_GATE // D_MODEL + 1)),
        row(ATT_GW),
        row(D_MODEL), full(gn), full(wa), full(wb), full(wo), full(nf), full(wr),
    ]
    return pl.pallas_call(
        _merge_kernel,
        grid=(n // tm,),
        in_specs=in_specs,
        out_specs=[
            row(D_MODEL),
            pl.BlockSpec((1, D_MODEL, tm), lambda i: (i // tpb, 0, i % tpb)),
            pl.BlockSpec((1, N_EXPERTS, tm), lambda i: (i // tpb, 0, i % tpb)),
        ],
        out_shape=[
            jax.ShapeDtypeStruct((n, D_MODEL), F32),
            jax.ShapeDtypeStruct((n // seq, D_MODEL, seq), BF16),
            jax.ShapeDtypeStruct((n // seq, N_EXPERTS, seq), F32),
        ],
        compiler_params=_cparams(("parallel",)),
        name="merge",
    )(of, ob, proj, proj, proj, y_b, x2d, gn, wa, wb, wo, nf, wr)


def _prefix_tables(n_exp, nch):
    rows = n_exp * nch
    tri = (np.arange(LANES)[:, None] <= np.arange(LANES)[None, :]).astype(np.float32)
    ri = np.arange(rows)
    same = (ri[:, None] // nch) == (ri[None, :] // nch)
    btri = np.logical_and(same, (ri[None, :] % nch) < (ri[:, None] % nch)).astype(np.float32)
    return jnp.asarray(tri, BF16), jnp.asarray(btri, BF16)


def _route_kernel(aff_ref, tri_ref, btri_ref, cin_ref, start_ref, *, cap):
    x = aff_ref[0]
    n_exp, nch, _ = x.shape
    rows = n_exp * nch
    bits = lax.bitcast_convert_type(x, jnp.int32)

    def body(it, ans):
        cand = jnp.bitwise_or(ans, lax.shift_left(jnp.int32(1), 30 - it))
        cnt = jnp.sum(jnp.where(bits >= cand, 1.0, 0.0), axis=(1, 2), keepdims=True)
        return jnp.where(cnt >= cap, cand, ans)

    thr = lax.fori_loop(0, 31, body, jnp.zeros((n_exp, 1, 1), jnp.int32))
    gt = jnp.where(bits > thr, 1.0, 0.0)
    eq = jnp.where(bits == thr, 1.0, 0.0)
    need = cap - jnp.sum(gt, axis=(1, 2), keepdims=True)
    need_rows = jnp.broadcast_to(need, x.shape).reshape(rows, LANES)
    gt2 = gt.reshape(rows, LANES)
    eq2 = eq.reshape(rows, LANES)

    def prefix(mask2):
        cin = jnp.dot(mask2.astype(BF16), tri_ref[...], preferred_element_type=F32)
        tot = jnp.broadcast_to(cin[:, LANES - 1:LANES], cin.shape).astype(BF16)
        off = jnp.dot(btri_ref[...], tot, preferred_element_type=F32)
        return cin, off

    cin_eq, off_eq = prefix(eq2)
    eq_before = off_eq + cin_eq - eq2
    sel = jnp.maximum(gt2, jnp.where(eq_before < need_rows, eq2, 0.0))
    cin, off = prefix(sel)
    cin_ref[0] = cin
    start_ref[0] = off


def _route(afft, cap):
    b, n_exp, t = afft.shape
    nch = t // LANES
    rows = n_exp * nch
    tri, btri = _prefix_tables(n_exp, nch)
    aff4 = afft.reshape(b, n_exp, nch, LANES)
    kern = functools.partial(_route_kernel, cap=cap)
    return pl.pallas_call(
        kern,
        grid=(b,),
        in_specs=[
            pl.BlockSpec((1, n_exp, nch, LANES), lambda i: (i, 0, 0, 0)),
            pl.BlockSpec((LANES, LANES), lambda i: (0, 0)),
            pl.BlockSpec((rows, rows), lambda i: (0, 0)),
        ],
        out_specs=[pl.BlockSpec((1, rows, LANES), lambda i: (i, 0, 0))] * 2,
        out_shape=[jax.ShapeDtypeStruct((b, rows, LANES), F32)] * 2,
        compiler_params=_cparams(("parallel",)),
        name="route",
    )(aff4, tri, btri)


def _pad_rows(a, rows):
    if a.shape[0] == rows:
        return a
    return jnp.concatenate([a, jnp.zeros((rows - a.shape[0], a.shape[1]), a.dtype)], axis=0)


def _compact_kernel(cin_ref, start_ref, aff_ref, idxc_ref, gatec_ref, idxr_ref, *, cap):
    r = _pad_rows(cin_ref[0, 0], LANES)
    st = _pad_rows(start_ref[0, 0], LANES)
    en = st + r[:, LANES - 1:LANES]
    st_row = st.T[0:1, :]
    en_row = en.T[0:1, :]
    sig = lax.broadcasted_iota(jnp.int32, (cap, LANES), 0).astype(F32)
    lane = lax.broadcasted_iota(jnp.int32, (cap, LANES), 1).astype(F32)
    onehot = jnp.where(jnp.logical_and(st_row <= sig, sig < en_row), 1.0, 0.0)
    st_sel = jnp.sum(onehot * st_row, axis=1, keepdims=True)
    c_sel = jnp.sum(onehot * lane, axis=1, keepdims=True)
    oh16 = onehot.astype(BF16)
    r_sel = jnp.dot(oh16, r.astype(BF16), preferred_element_type=F32)
    before = sig[:, 0:1] - st_sel
    pos = jnp.sum(jnp.where(r_sel <= before, 1.0, 0.0), axis=1, keepdims=True)
    idx = c_sel * LANES + pos
    idxc_ref[0] = idx.astype(jnp.int32)
    a = _pad_rows(aff_ref[0, 0], LANES)
    a1 = a.astype(BF16)
    rem = a - a1.astype(F32)
    a2 = rem.astype(BF16)
    a3 = (rem - a2.astype(F32)).astype(BF16)
    a_sel = (jnp.dot(oh16, a1, preferred_element_type=F32) + jnp.dot(oh16, a2, preferred_element_type=F32)
             + jnp.dot(oh16, a3, preferred_element_type=F32))
    gatec_ref[0] = jnp.sum(jnp.where(lane == pos, a_sel, 0.0), axis=1, keepdims=True)
    idx_b = jnp.broadcast_to(idx, (cap, LANES))
    for kb in range(cap // LANES):
        blk = idx_b[kb * LANES:(kb + 1) * LANES, :].T
        idxr_ref[0, :, kb * LANES:(kb + 1) * LANES] = blk[0:1, :].astype(jnp.int32)


def _compact(cin, start, afft, cap):
    b, n_exp, t = afft.shape
    nch = t // LANES
    cin4 = cin.reshape(b, n_exp, nch, LANES)
    start4 = start.reshape(b, n_exp, nch, LANES)
    aff4 = afft.reshape(b, n_exp, nch, LANES)
    in_spec = pl.BlockSpec((1, 1, nch, LANES), lambda bi, e: (bi, e, 0, 0))
    kern = functools.partial(_compact_kernel, cap=cap)
    return pl.pallas_call(
        kern,
        grid=(b, n_exp),
        in_specs=[in_spec, in_spec, in_spec],
        out_specs=[
            pl.BlockSpec((1, cap, 1), lambda bi, e: (bi * n_exp + e, 0, 0)),
            pl.BlockSpec((1, cap, 1), lambda bi, e: (bi * n_exp + e, 0, 0)),
            pl.BlockSpec((1, 1, cap), lambda bi, e: (bi * n_exp + e, 0, 0)),
        ],
        out_shape=[
            jax.ShapeDtypeStruct((b * n_exp, cap, 1), jnp.int32),
            jax.ShapeDtypeStruct((b * n_exp, cap, 1), F32),
            jax.ShapeDtypeStruct((b * n_exp, 1, cap), jnp.int32),
        ],
        compiler_params=_cparams(("parallel", "parallel")),
        name="compact",
    )(cin4, start4, aff4)


MOE_TS = 256
MOE_KG = 512
MOE_SW = 256
MOE_GK = 8


def _ffn_kernel(lo_ref, hi_ref, ht_ref, idxr_ref, gate_ref, wg_ref, wu_ref, wd_ref, y_ref, xet_ref, *, cap, kg):
    bi, e, ct = pl.program_id(0), pl.program_id(1), pl.program_id(2)
    n_exp = pl.num_programs(1)
    tm = xet_ref.shape[1]
    sw = min(MOE_SW, tm)
    cps = sw // LANES
    base = (bi * n_exp + e) * (cap // LANES) + ct * (tm // LANES)
    tok = lax.broadcasted_iota(jnp.int32, (kg, sw), 0)
    for c in range(tm // sw):
        lanes = slice(c * sw, (c + 1) * sw)
        ir = idxr_ref[0, :, lanes]
        xet_ref[:, lanes] = jnp.zeros((D_MODEL, sw), F32)

        def body(g, carry, lanes=lanes, ir=ir):
            t0 = pl.multiple_of(g * kg, kg)
            onehot_t = jnp.where(tok + t0 == ir, 1.0, 0.0).astype(BF16)
            xet_ref[:, lanes] += jnp.dot(ht_ref[0, :, pl.ds(t0, kg)], onehot_t, preferred_element_type=F32)
            return carry

        lax.fori_loop(lo_ref[base + c * cps], hi_ref[base + c * cps + cps - 1] + 1, body, 0)
    xe = xet_ref[...].T.astype(BF16)
    a = jnp.dot(xe, wg_ref[0], preferred_element_type=F32)
    u = jnp.dot(xe, wu_ref[0], preferred_element_type=F32)
    act = (a * (1.0 / (1.0 + jnp.exp(-a))) * u).astype(BF16)
    y = jnp.dot(act, wd_ref[0], preferred_element_type=F32)
    y_ref[0] = (y * gate_ref[0]).astype(BF16)


def _ffn(h2t, idx_col, idx_row, gate_col, w_gate, w_up, w_down, b, t, cap):
    n_exp = w_gate.shape[0]
    kg = min(MOE_KG, t)
    tm = min(512, cap)
    first = idx_col[:, 0::LANES, 0] // kg
    last = idx_col[:, LANES - 1::LANES, 0] // kg
    lo_tab = first.reshape(-1).astype(jnp.int32)
    hi_tab = last.reshape(-1).astype(jnp.int32)
    kern = functools.partial(_ffn_kernel, cap=cap, kg=kg)
    wspec = pl.BlockSpec((1, D_MODEL, EXPERT_FF), lambda bi, e, ct, lo, hi: (e, 0, 0))
    grid_spec = pltpu.PrefetchScalarGridSpec(
        num_scalar_prefetch=2,
        grid=(b, n_exp, cap // tm),
        in_specs=[
            pl.BlockSpec((1, D_MODEL, t), lambda bi, e, ct, lo, hi: (bi, 0, 0), pipeline_mode=pl.Buffered(1)),
            pl.BlockSpec((1, 1, tm), lambda bi, e, ct, lo, hi: (bi * n_exp + e, 0, ct)),
            pl.BlockSpec((1, tm, 1), lambda bi, e, ct, lo, hi: (bi * n_exp + e, ct, 0)),
            wspec, wspec,
            pl.BlockSpec((1, EXPERT_FF, D_MODEL), lambda bi, e, ct, lo, hi: (e, 0, 0)),
        ],
        out_specs=pl.BlockSpec((1, tm, D_MODEL), lambda bi, e, ct, lo, hi: (bi * n_exp + e, ct, 0)),
        scratch_shapes=[pltpu.VMEM((D_MODEL, tm), F32)],
    )
    return pl.pallas_call(
        kern,
        grid_spec=grid_spec,
        out_shape=jax.ShapeDtypeStruct((b * n_exp, cap, D_MODEL), BF16),
        compiler_params=_cparams(("arbitrary", "arbitrary", "arbitrary")),
        name="expert_ffn",
    )(lo_tab, hi_tab, h2t, idx_row, gate_col, w_gate, w_up, w_down)


def _combine_kernel(klo_ref, khi_ref, x1_ref, y_ref, idxr_ref, nfin_ref, o_ref, ystage_ref, istage_ref, acc_ref, *,
                    cap, final_norm):
    bi, ti = pl.program_id(0), pl.program_id(1)
    nts = pl.num_programs(1)
    ts2 = acc_ref.shape[0]
    n_exp = idxr_ref.shape[1]
    gk = MOE_GK * LANES

    @pl.when(jnp.logical_and(bi == 0, ti == 0))
    def _():
        ystage_ref[...] = jnp.zeros_like(ystage_ref)

    istage_ref[...] = jnp.full(istage_ref.shape, -1, jnp.int32)
    n = jnp.int32(0)
    for e in range(n_exp):
        tab = (bi * nts + ti) * n_exp + e

        def stage(k, n, e=e):
            s0 = pl.multiple_of(k * LANES, LANES)
            d0 = pl.multiple_of(n * LANES, LANES)
            ystage_ref[pl.ds(d0, LANES), :] = y_ref[0, pl.ds(e * cap + s0, LANES), :]
            istage_ref[:, pl.ds(d0, LANES)] = idxr_ref[0, e:e + 1, pl.ds(s0, LANES)]
            return n + 1

        n = lax.fori_loop(klo_ref[tab], khi_ref[tab] + 1, stage, n)
    tok = ti * ts2 + lax.broadcasted_iota(jnp.int32, (ts2, gk), 0)
    acc_ref[...] = jnp.zeros_like(acc_ref)

    def scatter(g, carry):
        k0 = pl.multiple_of(g * gk, gk)
        onehot = jnp.where(tok == istage_ref[:, pl.ds(k0, gk)], 1.0, 0.0).astype(BF16)
        acc_ref[...] += jnp.dot(onehot, ystage_ref[pl.ds(k0, gk), :], preferred_element_type=F32)
        return carry

    lax.fori_loop(0, (n + MOE_GK - 1) // MOE_GK, scatter, 0)
    out = x1_ref[...] + acc_ref[...]
    if final_norm:
        out = out * lax.rsqrt(jnp.mean(out * out, axis=-1, keepdims=True) + RMS_EPS) * nfin_ref[...]
    o_ref[...] = out


def _combine(x1, y, idx_row, start, norm_final, b, t, cap, final_norm):
    n_exp = N_EXPERTS
    ts2 = min(MOE_TS, t)
    nts = t // ts2
    nch = t // LANES
    st = start[:, :, 0].reshape(b, n_exp, nch)[:, :, ::ts2 // LANES].astype(jnp.int32)
    en = jnp.concatenate([st[:, :, 1:], jnp.full((b, n_exp, 1), cap, jnp.int32)], axis=2)
    klo = st // LANES
    khi = jnp.where(en > st, (en - 1) // LANES, klo - 1)
    klo_tab = jnp.transpose(klo, (0, 2, 1)).reshape(-1)
    khi_tab = jnp.transpose(khi, (0, 2, 1)).reshape(-1)
    y3 = y.reshape(b, n_exp * cap, D_MODEL)
    idxr3 = idx_row.reshape(b, n_exp, cap)
    max_chunks = n_exp * min(ts2 // LANES + 1, cap // LANES)
    stage_chunks = -(-max_chunks // MOE_GK) * MOE_GK
    kern = functools.partial(_combine_kernel, cap=cap, final_norm=final_norm)
    grid_spec = pltpu.PrefetchScalarGridSpec(
        num_scalar_prefetch=2,
        grid=(b, nts),
        in_specs=[
            pl.BlockSpec((ts2, D_MODEL), lambda bi, ti, lo, hi: (bi * nts + ti, 0)),
            pl.BlockSpec((1, n_exp * cap, D_MODEL), lambda bi, ti, lo, hi: (bi, 0, 0), pipeline_mode=pl.Buffered(1)),
            pl.BlockSpec((1, n_exp, cap), lambda bi, ti, lo, hi: (bi, 0, 0)),
            pl.BlockSpec((1, D_MODEL), lambda bi, ti, lo, hi: (0, 0)),
        ],
        out_specs=pl.BlockSpec((ts2, D_MODEL), lambda bi, ti, lo, hi: (bi * nts + ti, 0)),
        scratch_shapes=[pltpu.VMEM((stage_chunks * LANES, D_MODEL), BF16),
                        pltpu.VMEM((1, stage_chunks * LANES), jnp.int32),
                        pltpu.VMEM((ts2, D_MODEL), F32)],
    )
    return pl.pallas_call(
        kern,
        grid_spec=grid_spec,
        out_shape=jax.ShapeDtypeStruct((b * t, D_MODEL), F32),
        compiler_params=_cparams(("arbitrary", "arbitrary")),
        name="moe_combine",
    )(klo_tab, khi_tab, x1, y3, idxr3, norm_final.reshape(1, D_MODEL))


GDN_CHUNK = 64


def kernel(x, norm_mix, w_in, conv_w, a_log, dt_bias, gdn_norm, w_branch_a, w_branch_b, w_out, norm_ffn, w_router, w_expert_gate, w_expert_up, w_expert_down, norm_final):
    b, t, d = x.shape
    depth = w_in.shape[0]
    cap = EC_CAPACITY_FACTOR * t // N_EXPERTS
    rope_cos, rope_sin = _rope_tables(t)
    xc = x.reshape(b * t, d)
    for layer in range(depth):
        w_all, w_bd, w_att = _proj_weights(w_in[layer])
        nw = norm_mix[layer].reshape(1, d)
        proj, bd = _proj(xc, nw, w_all, w_bd)
        att_qkv = _att_proj(xc, nw, w_att, rope_cos, rope_sin, b, t)
        proj3 = proj.reshape(b, t, PROJ_W)
        qkv, aux, gcr = _gdn_prep(proj3, bd.reshape(b, t, LANES), conv_w[layer], a_log[layer], dt_bias[layer],
                                  GDN_CHUNK)
        o_f, o_b = _gdn_scan(qkv, aux, gcr, GDN_CHUNK)
        y_b = _attention(att_qkv, b, t)
        x1, h2t, afft = _merge(o_f.reshape(b * t, GDN_W), o_b.reshape(b * t, GDN_W), proj, y_b, xc,
                              gdn_norm[layer], w_branch_a[layer].astype(BF16), w_branch_b[layer].astype(BF16),
                              w_out[layer].astype(BF16), norm_ffn[layer], w_router[layer], t)
        cin, start = _route(afft, cap)
        idx_col, gate_col, idx_row = _compact(cin, start, afft, cap)
        y = _ffn(h2t, idx_col, idx_row, gate_col, w_expert_gate[layer].astype(BF16),
                 w_expert_up[layer].astype(BF16), w_expert_down[layer].astype(BF16), b, t, cap)
        xc = _combine(x1, y, idx_row, start, norm_final, b, t, cap, final_norm=(layer == depth - 1))
    return xc.reshape(b, t, d)
```

```python
import functools

import numpy as np
import jax
import jax.numpy as jnp
from jax import lax
from jax.experimental import pallas as pl
from jax.experimental.pallas import tpu as pltpu

F32 = jnp.float32
BF16 = jnp.bfloat16

D_MODEL = 1024
GDN_HEADS = 4
GDN_DK = 128
GDN_DV = 128
CONV_WIDTH = 5
ATT_GROUPS = ((128, 1), (512, 4), (2048, 16))
ATT_HPG = 4
ATT_HEAD_DIM = 64
ATT_SPAN = 64
ROPE_THETA = 500000.0
ROPE_DIMS = ATT_HEAD_DIM // 4
N_EXPERTS = 16
EXPERT_FF = 1024
EC_CAPACITY_FACTOR = 2
RMS_EPS = 1e-6
NEG_BIG = -1e30

GDN_W = GDN_HEADS * GDN_DK
GDN_CONV_CH = 3 * GDN_W
ATT_GW = ATT_HPG * ATT_HEAD_DIM
ATT_W = len(ATT_GROUPS) * ATT_GW

C_QKVA = 0
C_Z = C_QKVA + GDN_CONV_CH
C_GATE = C_Z + GDN_W
PROJ_W = C_GATE + 2 * D_MODEL
PROJ_TN = 512
BD_W = 4 * GDN_HEADS

LANES = 128
VMEM_LIMIT = 56 * 1024 * 1024


def _cparams(sem):
    return pltpu.CompilerParams(dimension_semantics=sem, vmem_limit_bytes=VMEM_LIMIT)


def _proj_kernel(x_ref, nw_ref, w_ref, wbd_ref, o_ref, bd_ref):
    x = x_ref[...]
    ms = jnp.mean(x * x, axis=-1, keepdims=True)
    h = (x * lax.rsqrt(ms + RMS_EPS) * nw_ref[...]).astype(BF16)
    for j in range(PROJ_W // PROJ_TN):
        cols = slice(j * PROJ_TN, (j + 1) * PROJ_TN)
        o_ref[:, cols] = jnp.dot(h, w_ref[:, cols], preferred_element_type=F32).astype(BF16)
    bd_ref[...] = jnp.dot(h, wbd_ref[...], preferred_element_type=F32)


def _proj(x2d, norm_w, w_all, w_bd):
    n = x2d.shape[0]
    tm = min(512, n)
    return pl.pallas_call(
        _proj_kernel,
        grid=(n // tm,),
        in_specs=[
            pl.BlockSpec((tm, D_MODEL), lambda i: (i, 0)),
            pl.BlockSpec((1, D_MODEL), lambda i: (0, 0)),
            pl.BlockSpec((D_MODEL, PROJ_W), lambda i: (0, 0), pipeline_mode=pl.Buffered(1)),
            pl.BlockSpec((D_MODEL, LANES), lambda i: (0, 0)),
        ],
        out_specs=[pl.BlockSpec((tm, PROJ_W), lambda i: (i, 0)), pl.BlockSpec((tm, LANES), lambda i: (i, 0))],
        out_shape=[jax.ShapeDtypeStruct((n, PROJ_W), BF16), jax.ShapeDtypeStruct((n, LANES), F32)],
        compiler_params=_cparams(("parallel",)),
        name="proj",
    )(x2d, norm_w, w_all, w_bd)


def _att_proj_kernel(x_ref, nw_ref, w_ref, rc_ref, rs_ref, o0_ref, o1_ref, o2_ref, scr_ref):
    x = x_ref[...]
    tm = x.shape[0]
    ms = jnp.mean(x * x, axis=-1, keepdims=True)
    h = (x * lax.rsqrt(ms + RMS_EPS) * nw_ref[...]).astype(BF16)
    reps = ATT_GW // LANES
    cos = jnp.concatenate([rc_ref[...]] * reps, axis=1)
    sin = jnp.concatenate([rs_ref[...]] * reps, axis=1)
    lane = lax.broadcasted_iota(jnp.int32, (tm, ATT_GW), 1) % ATT_HEAD_DIM
    half = ROPE_DIMS // 2
    outs = (o0_ref, o1_ref, o2_ref)
    for g, (_, r) in enumerate(ATT_GROUPS):
        for which in range(3):
            c0 = which * ATT_W + g * ATT_GW
            acc = jnp.dot(h, w_ref[:, c0:c0 + ATT_GW], preferred_element_type=F32)
            if which < 2:
                partner = jnp.where(lane < half, pltpu.roll(acc, ATT_GW - half, 1), pltpu.roll(acc, half, 1))
                acc = acc * cos + partner * sin
            cols = slice(which * ATT_GW, (which + 1) * ATT_GW)
            if r == 1:
                outs[g][0, 0, :, cols] = acc.astype(BF16)
            else:
                for hf in range(reps):
                    scr_ref[hf] = acc[:, hf * LANES:(hf + 1) * LANES]
                for s in range(r):
                    sub = [scr_ref[hf, pl.ds(s, tm // r, stride=r), :] for hf in range(reps)]
                    outs[g][0, s, :, cols] = jnp.concatenate(sub, axis=1).astype(BF16)


def _att_proj(x2d, norm_w, w_att, rope_cos, rope_sin, b, seq):
    n = x2d.shape[0]
    tm = min(1024, seq)
    tpb = seq // tm
    out_specs, out_shape = [], []
    for _, r in ATT_GROUPS:
        out_specs.append(pl.BlockSpec((1, r, tm // r, 3 * ATT_GW), lambda i: (i // tpb, 0, i % tpb, 0)))
        out_shape.append(jax.ShapeDtypeStruct((b, r, seq // r, 3 * ATT_GW), BF16))
    return pl.pallas_call(
        _att_proj_kernel,
        grid=(n // tm,),
        in_specs=[
            pl.BlockSpec((tm, D_MODEL), lambda i: (i, 0)),
            pl.BlockSpec((1, D_MODEL), lambda i: (0, 0)),
            pl.BlockSpec((D_MODEL, 3 * ATT_W), lambda i: (0, 0)),
            pl.BlockSpec((tm, LANES), lambda i: (i % tpb, 0)),
            pl.BlockSpec((tm, LANES), lambda i: (i % tpb, 0)),
        ],
        out_specs=out_specs,
        out_shape=out_shape,
        scratch_shapes=[pltpu.VMEM((ATT_GW // LANES, tm, LANES), F32)],
        compiler_params=_cparams(("parallel",)),
        name="att_proj",
    )(x2d, norm_w, w_att, rope_cos, rope_sin)


def _rope_tables(seq):
    half = ROPE_DIMS // 2
    inv_freq = jnp.power(ROPE_THETA, -jnp.arange(half, dtype=F32) * 2.0 / ROPE_DIMS)
    ang = jnp.arange(seq, dtype=F32)[:, None] * inv_freq[None, :]
    cos, sin = jnp.cos(ang), jnp.sin(ang)
    rest = ATT_HEAD_DIM - ROPE_DIMS
    cos_h = jnp.concatenate([cos, cos, jnp.ones((seq, rest), F32)], axis=1)
    sin_h = jnp.concatenate([-sin, sin, jnp.zeros((seq, rest), F32)], axis=1)
    reps = LANES // ATT_HEAD_DIM
    return jnp.tile(cos_h, (1, reps)), jnp.tile(sin_h, (1, reps))


def _proj_weights(w_in):
    off_z = GDN_CONV_CH
    off_beta = off_z + GDN_W
    off_att = off_beta + 4 * GDN_HEADS
    off_gate = off_att + 3 * ATT_W
    qkva = w_in[:, :off_z]
    z = w_in[:, off_z:off_beta]
    bd = w_in[:, off_beta:off_att]
    att = w_in[:, off_att:off_gate]
    gates = w_in[:, off_gate:]
    att = jnp.concatenate([att[:, :ATT_W] * (ATT_HEAD_DIM ** -0.5), att[:, ATT_W:]], axis=1)
    bd = jnp.concatenate([bd, jnp.zeros((w_in.shape[0], LANES - BD_W), w_in.dtype)], axis=1)
    return jnp.concatenate([qkva, z, gates], axis=1).astype(BF16), bd.astype(BF16), att.astype(BF16)


AUX_BETA = 0
AUX_GC = 8
AUX_EG = 16
AUX_EK = 24
AUX_GE = 32
HALO = 16


def _gdn_prep_kernel(xp_ref, xc_ref, xn_ref, bd_ref, cw_ref, alog_ref, dtb_ref,
                     qkv_ref, aux_ref, gcr_ref, xf_ref, *, chunk):
    i = pl.program_id(1)
    nblk = pl.num_programs(1)
    tp = xc_ref.shape[1]
    prev = jnp.where(i > 0, xp_ref[0].astype(F32), 0.0)
    nxt = jnp.where(i < nblk - 1, xn_ref[0].astype(F32), 0.0)
    xf_ref[0:HALO, :] = prev
    xf_ref[HALO:HALO + tp, :] = xc_ref[0].astype(F32)
    xf_ref[HALO + tp:, :] = nxt
    pad = (CONV_WIDTH - 1) // 2
    for hb in range(3 * GDN_HEADS):
        sl = slice(hb * GDN_DK, (hb + 1) * GDN_DK)
        conv = None
        for j in range(CONV_WIDTH):
            term = xf_ref[pl.ds(HALO - pad + j, tp), sl] * cw_ref[j:j + 1, sl]
            conv = term if conv is None else conv + term
        uh = conv * (1.0 / (1.0 + jnp.exp(-conv)))
        if hb < 2 * GDN_HEADS:
            scale = lax.rsqrt(jnp.sum(uh * uh, axis=-1, keepdims=True) + 1e-6)
            if hb < GDN_HEADS:
                scale = scale * (GDN_DK ** -0.5)
            uh = uh * scale
        qkv_ref[0, :, sl] = uh.astype(BF16)

    nh2 = 2 * GDN_HEADS
    bd = bd_ref[0]
    lane = lax.broadcasted_iota(jnp.int32, bd.shape, 1)
    row = lax.broadcasted_iota(jnp.int32, bd.shape, 0) % chunk
    beta = 1.0 / (1.0 + jnp.exp(-bd))
    sp_in = bd + dtb_ref[...]
    softplus = jnp.maximum(sp_in, 0.0) + jnp.log(1.0 + jnp.exp(-jnp.abs(sp_in)))
    g = -jnp.exp(alog_ref[...]) * softplus
    g = jnp.where(jnp.logical_and(lane >= nh2, lane < 2 * nh2), g, 0.0)
    pre = g
    suf = g
    s = 1
    while s < chunk:
        pre = pre + jnp.where(row >= s, pltpu.roll(pre, s, 0), 0.0)
        suf = suf + jnp.where(row < chunk - s, pltpu.roll(suf, tp - s, 0), 0.0)
        s *= 2
    total = pre + suf - g
    is_fwd = lane < nh2 + GDN_HEADS
    gc = jnp.where(is_fwd, pre, suf)
    eg = jnp.exp(gc)
    ek = jnp.exp(total - gc)
    ge = jnp.exp(total)
    aux = jnp.where(lane < nh2, beta, 0.0)
    aux = aux + gc
    aux = aux + jnp.where(jnp.logical_and(lane >= AUX_EG, lane < AUX_EG + nh2), pltpu.roll(eg, AUX_EG - AUX_GC, 1), 0.0)
    aux = aux + jnp.where(jnp.logical_and(lane >= AUX_EK, lane < AUX_EK + nh2), pltpu.roll(ek, AUX_EK - AUX_GC, 1), 0.0)
    aux = aux + jnp.where(jnp.logical_and(lane >= AUX_GE, lane < AUX_GE + nh2), pltpu.roll(ge, AUX_GE - AUX_GC, 1), 0.0)
    aux_ref[0] = aux
    gct = gc.T
    for c in range(tp // chunk):
        gcr_ref[0, c] = gct[AUX_GC:AUX_GC + nh2, c * chunk:(c + 1) * chunk]


def _gdn_prep(proj3, bd3, conv_w, a_log, dt_bias, chunk):
    b, t, _ = proj3.shape
    tp = min(256, t)
    nblk = t // tp
    nh2 = 2 * GDN_HEADS
    alog_row = jnp.zeros((1, LANES), F32).at[0, nh2:2 * nh2].set(a_log.reshape(-1))
    dtb_row = jnp.zeros((1, LANES), F32).at[0, nh2:2 * nh2].set(dt_bias.reshape(-1))
    hb = tp // HALO
    kern = functools.partial(_gdn_prep_kernel, chunk=chunk)
    return pl.pallas_call(
        kern,
        grid=(b, nblk),
        in_specs=[
            pl.BlockSpec((1, HALO, GDN_CONV_CH), lambda bi, i: (bi, jnp.maximum(i * hb - 1, 0), 0)),
            pl.BlockSpec((1, tp, GDN_CONV_CH), lambda bi, i: (bi, i, 0)),
            pl.BlockSpec((1, HALO, GDN_CONV_CH), lambda bi, i: (bi, jnp.minimum((i + 1) * hb, t // HALO - 1), 0)),
            pl.BlockSpec((1, tp, LANES), lambda bi, i: (bi, i, 0)),
            pl.BlockSpec((CONV_WIDTH, GDN_CONV_CH), lambda bi, i: (0, 0)),
            pl.BlockSpec((1, LANES), lambda bi, i: (0, 0)),
            pl.BlockSpec((1, LANES), lambda bi, i: (0, 0)),
        ],
        out_specs=[
            pl.BlockSpec((1, tp, GDN_CONV_CH), lambda bi, i: (bi, i, 0)),
            pl.BlockSpec((1, tp, LANES), lambda bi, i: (bi, i, 0)),
            pl.BlockSpec((1, tp // chunk, nh2, chunk), lambda bi, i: (bi, i, 0, 0)),
        ],
        out_shape=[
            jax.ShapeDtypeStruct((b, t, GDN_CONV_CH), BF16),
            jax.ShapeDtypeStruct((b, t, LANES), F32),
            jax.ShapeDtypeStruct((b, t // chunk, nh2, chunk), F32),
        ],
        scratch_shapes=[pltpu.VMEM((tp + 2 * HALO, GDN_CONV_CH), F32)],
        compiler_params=_cparams(("parallel", "parallel")),
        name="gdn_prep",
    )(proj3, proj3, proj3, bd3, conv_w, alog_row, dtb_row)


GDN_CHUNKS_PER_ITER = 2
_DN_T = (((1,), (1,)), ((), ()))
_DN_L = (((0,), (0,)), ((), ()))


def _gdn_intra_step(units, chunk):
    ri = lax.broadcasted_iota(jnp.int32, (chunk, chunk), 0)
    ci = lax.broadcasted_iota(jnp.int32, (chunk, chunk), 1)
    eye = jnp.where(ri == ci, 1.0, 0.0)
    for un in units:
        col, aux = un["col"], un["aux"]
        beta = aux[:, AUX_BETA + col:AUX_BETA + col + 1]
        gc = aux[:, AUX_GC + col:AUX_GC + col + 1]
        eg = aux[:, AUX_EG + col:AUX_EG + col + 1]
        ek = aux[:, AUX_EK + col:AUX_EK + col + 1]
        un["ge"] = aux[0:1, AUX_GE + col:AUX_GE + col + 1]
        if un["d"] == 0:
            incl, strict = ri >= ci, ri > ci
        else:
            incl, strict = ri <= ci, ri < ci
        un["strict"] = strict
        un["decay"] = jnp.where(incl, jnp.exp(jnp.minimum(gc - un["gcr"], 0.0)), 0.0)
        kb = un["k"] * beta
        un["kbq"] = jnp.concatenate([kb, un["q"]], axis=0).astype(BF16)
        un["kf"] = un["k"].astype(BF16)
        un["rhs"] = jnp.concatenate([un["v"] * beta, kb * eg], axis=1).astype(BF16)
        un["qd"] = un["q"] * eg
        un["kd"] = (un["k"] * ek).astype(BF16)
    for un in units:
        un["kkqk"] = lax.dot_general(un["kbq"], un["kf"], _DN_T, preferred_element_type=F32)
    for un in units:
        lmat = jnp.where(un["strict"], un["kkqk"][:chunk] * un["decay"], 0.0)
        un["amat"] = (un["kkqk"][chunk:] * un["decay"]).astype(BF16)
        un["pw"] = lmat
        un["tinv"] = eye - lmat
    n_sq = max(chunk.bit_length() - 2, 0)
    for i in range(n_sq + 1):
        for un in units:
            pw16 = un["pw"].astype(BF16)
            if i == 0:
                un["prod"] = jnp.dot(pw16, pw16, preferred_element_type=F32)
            elif i < n_sq:
                lhs = jnp.concatenate([un["pw"], un["tinv"]], axis=0).astype(BF16)
                un["prod"] = jnp.dot(lhs, pw16, preferred_element_type=F32)
            else:
                un["prod"] = jnp.dot(un["tinv"].astype(BF16), pw16, preferred_element_type=F32)
        for un in units:
            if i == 0:
                un["pw"] = un["prod"]
            elif i < n_sq:
                un["tinv"] = un["tinv"] + un["prod"][chunk:]
                un["pw"] = un["prod"][:chunk]
            else:
                un["tinv"] = un["tinv"] + un["prod"]
    for un in units:
        un["uw"] = jnp.dot(un["tinv"].astype(BF16), un["rhs"], preferred_element_type=F32)


def _gdn_state_step(units, chunk):
    for un in units:
        lhs = jnp.concatenate([un["uw"][:, GDN_DV:], un["qd"]], axis=0).astype(BF16)
        un["ws"] = jnp.dot(lhs, un["s"].astype(BF16), preferred_element_type=F32)
    for un in units:
        un["v_new"] = (un["uw"][:, :GDN_DV] - un["ws"][:chunk]).astype(BF16)
    for un in units:
        un["o"] = un["ws"][chunk:] + jnp.dot(un["amat"], un["v_new"], preferred_element_type=F32)
        un["s_new"] = un["s"] * un["ge"] + lax.dot_general(un["kd"], un["v_new"], _DN_L, preferred_element_type=F32)


def _gdn_scan_kernel(qkvf_ref, qkvb_ref, auxf_ref, auxb_ref, gcrf_ref, gcrb_ref,
                     of_ref, ob_ref, s_ref, *, chunk):
    @pl.when(pl.program_id(1) == 0)
    def _():
        s_ref[...] = jnp.zeros_like(s_ref)

    nc = qkvf_ref.shape[1] // chunk
    dirs = ((qkvf_ref, auxf_ref, gcrf_ref, of_ref), (qkvb_ref, auxb_ref, gcrb_ref, ob_ref))

    per_iter = GDN_CHUNKS_PER_ITER if nc % GDN_CHUNKS_PER_ITER == 0 else 1

    def body(it, carry):
        groups = []
        for sub in range(per_iter):
            c = it * per_iter + sub
            units = []
            for d, (qkv_ref, aux_ref, gcr_ref, o_ref) in enumerate(dirs):
                cc = c if d == 0 else nc - 1 - c
                r0 = pl.multiple_of(cc * chunk, chunk)
                aux = aux_ref[0, pl.ds(r0, chunk), :]
                gcr_all = gcr_ref[0, cc]
                for h in range(GDN_HEADS):
                    si = d * GDN_HEADS + h
                    units.append(dict(
                        q=qkv_ref[0, pl.ds(r0, chunk), h * GDN_DK:(h + 1) * GDN_DK],
                        k=qkv_ref[0, pl.ds(r0, chunk), GDN_W + h * GDN_DK:GDN_W + (h + 1) * GDN_DK],
                        v=qkv_ref[0, pl.ds(r0, chunk), 2 * GDN_W + h * GDN_DV:2 * GDN_W + (h + 1) * GDN_DV],
                        aux=aux, gcr=gcr_all[si:si + 1, :], col=si, d=d, h=h, r0=r0, o_ref=o_ref))
            groups.append(units)
        _gdn_intra_step([un for units in groups for un in units], chunk)
        state = [s_ref[si] for si in range(2 * GDN_HEADS)]
        for units in groups:
            for un in units:
                un["s"] = state[un["col"]]
            _gdn_state_step(units, chunk)
            for un in units:
                state[un["col"]] = un["s_new"]
                un["o_ref"][0, pl.ds(un["r0"], chunk), un["h"] * GDN_DV:(un["h"] + 1) * GDN_DV] = un["o"]
        for si in range(2 * GDN_HEADS):
            s_ref[si] = state[si]
        return carry

    lax.fori_loop(0, nc // per_iter, body, 0)


def _gdn_scan(qkv, aux, gcr, chunk):
    b, t, _ = qkv.shape
    tb = min(512, t)
    nb = t // tb
    nh2 = 2 * GDN_HEADS
    kern = functools.partial(_gdn_scan_kernel, chunk=chunk)
    fwd = lambda bi, i: (bi, i, 0)
    bwd = lambda bi, i: (bi, nb - 1 - i, 0)
    return pl.pallas_call(
        kern,
        grid=(b, nb),
        in_specs=[
            pl.BlockSpec((1, tb, GDN_CONV_CH), fwd),
            pl.BlockSpec((1, tb, GDN_CONV_CH), bwd),
            pl.BlockSpec((1, tb, LANES), fwd),
            pl.BlockSpec((1, tb, LANES), bwd),
            pl.BlockSpec((1, tb // chunk, nh2, chunk), lambda bi, i: (bi, i, 0, 0)),
            pl.BlockSpec((1, tb // chunk, nh2, chunk), lambda bi, i: (bi, nb - 1 - i, 0, 0)),
        ],
        out_specs=[pl.BlockSpec((1, tb, GDN_W), fwd), pl.BlockSpec((1, tb, GDN_W), bwd)],
        out_shape=[jax.ShapeDtypeStruct((b, t, GDN_W), F32)] * 2,
        scratch_shapes=[pltpu.VMEM((nh2, GDN_DK, GDN_DV), F32)],
        compiler_params=_cparams(("parallel", "arbitrary")),
        name="gdn_scan",
    )(qkv, qkv, aux, aux, gcr, gcr)


ATT_QT = 128


ATT_BLK = 2048


ATT_TILES_PER_ITER = 2


def _attn_tiles(tiles):
    head_of_lane = lax.broadcasted_iota(jnp.int32, (ATT_QT, ATT_GW), 1) // ATT_HEAD_DIM
    in_head = [head_of_lane == h for h in range(ATT_HPG)]
    for tl in tiles:
        q = tl["q"]
        q4 = jnp.concatenate([jnp.where(m, q, jnp.zeros_like(q)) for m in in_head], axis=0)
        tl["s"] = lax.dot_general(q4, tl["kt"], _DN_T, preferred_element_type=F32)
    for tl in tiles:
        nk = tl["s"].shape[1]
        s = jnp.where(tl["valid"][None], tl["s"].reshape(ATT_HPG, ATT_QT, nk), NEG_BIG)
        mx = jnp.max(s, axis=-1, keepdims=True)
        p = jnp.exp(s - mx)
        tl["l"] = jnp.sum(p, axis=-1, keepdims=True)
        tl["mx"] = mx
        tl["p"] = p.reshape(ATT_HPG * ATT_QT, nk).astype(BF16)
    for tl in tiles:
        tl["pv"] = jnp.dot(tl["p"], tl["vt"], preferred_element_type=F32)
    for tl in tiles:
        o = jnp.zeros((ATT_QT, ATT_GW), F32)
        lse = jnp.zeros((ATT_QT, ATT_GW), F32)
        for h in range(ATT_HPG):
            rows = slice(h * ATT_QT, (h + 1) * ATT_QT)
            o = jnp.where(in_head[h], tl["pv"][rows] / tl["l"][h], o)
            lse = jnp.where(in_head[h], tl["mx"][h] + jnp.log(tl["l"][h]), lse)
        tl["o"], tl["lse"] = o, lse


def _attn_kernel(*refs, seq):
    n_g = len(ATT_GROUPS)
    ins = refs[:7 * n_g]
    y_ref = refs[7 * n_g]
    kv_scr = refs[7 * n_g + 1:7 * n_g + 1 + 2 * n_g]
    o_scr, l_scr = refs[7 * n_g + 1 + 2 * n_g:]
    i = pl.program_id(1)
    n_half = ATT_GW // LANES
    nk = ATT_QT + 2 * ATT_SPAN
    qi = lax.broadcasted_iota(jnp.int32, (ATT_QT, nk), 0)
    ki = lax.broadcasted_iota(jnp.int32, (ATT_QT, nk), 1)
    rel = ki - qi
    band = jnp.logical_and(rel >= 0, rel <= 2 * ATT_SPAN)
    for g, (_, r) in enumerate(ATT_GROUPS):
        q_ref, kp_ref, kc_ref, kn_ref, vp_ref, vc_ref, vn_ref = ins[7 * g:7 * g + 7]
        kf_ref, vf_ref = kv_scr[2 * g:2 * g + 2]
        rows = ATT_BLK // r
        m_total = seq // r
        for full_ref, parts in ((kf_ref, (kp_ref, kc_ref, kn_ref)), (vf_ref, (vp_ref, vc_ref, vn_ref))):
            full_ref[:, 0:ATT_SPAN, :] = parts[0][0]
            full_ref[:, ATT_SPAN:ATT_SPAN + rows, :] = parts[1][0]
            full_ref[:, ATT_SPAN + rows:, :] = parts[2][0]
        tiles = rows // ATT_QT

        assert (r * tiles) % ATT_TILES_PER_ITER == 0

        def tile_group(it, carry, r=r, g=g, rows=rows, tiles=tiles, m_total=m_total,
                       q_ref=q_ref, kf_ref=kf_ref, vf_ref=vf_ref):
            group = []
            for u in range(ATT_TILES_PER_ITER):
                idx = it * ATT_TILES_PER_ITER + u
                s = idx // tiles
                r0 = pl.multiple_of((idx % tiles) * ATT_QT, ATT_QT)
                mk = i * rows + r0 - ATT_SPAN + ki
                group.append(dict(
                    res=s, r0=r0, q=q_ref[0, s, pl.ds(r0, ATT_QT), :],
                    kt=kf_ref[s, pl.ds(r0, nk), :], vt=vf_ref[s, pl.ds(r0, nk), :],
                    valid=jnp.logical_and(band, jnp.logical_and(mk >= 0, mk < m_total))))
            _attn_tiles(group)
            for tl in group:
                for hf in range(n_half):
                    lanes = slice(hf * LANES, (hf + 1) * LANES)
                    if r == 1:
                        dst = pl.ds(tl["r0"], ATT_QT)
                    else:
                        dst = pl.ds(tl["r0"] * r + tl["res"], ATT_QT, stride=r)
                    o_scr[g * n_half + hf, dst, :] = tl["o"][:, lanes]
                    l_scr[g * n_half + hf, dst, :] = tl["lse"][:, lanes]
            return carry

        lax.fori_loop(0, r * tiles // ATT_TILES_PER_ITER, tile_group, 0)

    cb = 256

    def comb(c, carry):
        r0 = pl.multiple_of(c * cb, cb)
        halves = []
        for hf in range(n_half):
            ls = [l_scr[g * n_half + hf, pl.ds(r0, cb), :] for g in range(n_g)]
            lm = functools.reduce(jnp.maximum, ls)
            es = [jnp.exp(lv - lm) for lv in ls]
            num = sum(es[g] * o_scr[g * n_half + hf, pl.ds(r0, cb), :] for g in range(n_g))
            halves.append(num / sum(es))
        y_ref[pl.ds(r0, cb), :] = jnp.concatenate(halves, axis=1).astype(BF16)
        return carry

    lax.fori_loop(0, ATT_BLK // cb, comb, 0)


def _attention(att_qkv, b, seq):
    assert seq % ATT_BLK == 0
    nblk = seq // ATT_BLK
    in_specs, scratch, args = [], [], []
    for (_, r), arr in zip(ATT_GROUPS, att_qkv):
        rows = ATT_BLK // r
        assert rows % ATT_QT == 0
        hb = rows // ATT_SPAN
        last = seq // r // ATT_SPAN - 1

        def center(c, r=r, rows=rows):
            return pl.BlockSpec((1, r, rows, ATT_GW), lambda bi, i: (bi, 0, i, c))

        def prev(c, r=r, hb=hb):
            return pl.BlockSpec((1, r, ATT_SPAN, ATT_GW), lambda bi, i: (bi, 0, jnp.maximum(i * hb - 1, 0), c))

        def nxt(c, r=r, hb=hb, last=last):
            return pl.BlockSpec((1, r, ATT_SPAN, ATT_GW), lambda bi, i: (bi, 0, jnp.minimum((i + 1) * hb, last), c))

        in_specs += [center(0), prev(1), center(1), nxt(1), prev(2), center(2), nxt(2)]
        args += [arr] * 7
        scratch += [pltpu.VMEM((r, rows + 2 * ATT_SPAN, ATT_GW), BF16)] * 2
    n_g = len(ATT_GROUPS)
    scratch += [pltpu.VMEM((n_g * (ATT_GW // LANES), ATT_BLK, LANES), F32)] * 2
    kern = functools.partial(_attn_kernel, seq=seq)
    return pl.pallas_call(
        kern,
        grid=(b, nblk),
        in_specs=in_specs,
        out_specs=pl.BlockSpec((ATT_BLK, ATT_GW), lambda bi, i: (bi * nblk + i, 0)),
        out_shape=jax.ShapeDtypeStruct((b * seq, ATT_GW), BF16),
        scratch_shapes=scratch,
        compiler_params=_cparams(("parallel", "parallel")),
        name="attention",
    )(*args)


def _merge_kernel(of_ref, ob_ref, z_ref, ga_ref, gb_ref, yb_ref,
                  x_ref, gn_ref, wa_ref, wb_ref, wo_ref, nf_ref, wr_ref,
                  x1_ref, h2r_ref, afft_ref):
    o = of_ref[...] + ob_ref[...]
    z = z_ref[...].astype(F32)
    ya_parts = []
    for h in range(GDN_HEADS):
        sl = slice(h * GDN_DV, (h + 1) * GDN_DV)
        oh = o[:, sl]
        oh = oh * lax.rsqrt(jnp.mean(oh * oh, axis=-1, keepdims=True) + RMS_EPS) * gn_ref[...]
        zh = z[:, sl]
        ya_parts.append(oh * (zh * (1.0 / (1.0 + jnp.exp(-zh)))))
    y_a = jnp.concatenate(ya_parts, axis=1).astype(BF16)
    y_b = yb_ref[...]
    ga = 1.0 / (1.0 + jnp.exp(-ga_ref[...].astype(F32)))
    gb = 1.0 / (1.0 + jnp.exp(-gb_ref[...].astype(F32)))
    mixed = (ga * jnp.dot(y_a, wa_ref[...], preferred_element_type=F32)
             + gb * jnp.dot(y_b, wb_ref[...], preferred_element_type=F32))
    x1 = x_ref[...] + jnp.dot(mixed.astype(BF16), wo_ref[...], preferred_element_type=F32)
    x1_ref[...] = x1
    h2f = x1 * lax.rsqrt(jnp.mean(x1 * x1, axis=-1, keepdims=True) + RMS_EPS) * nf_ref[...]
    h2 = h2f.astype(BF16)
    tm = h2f.shape[0]
    for j in range(D_MODEL // LANES):
        h2r_ref[pl.ds(j, tm, stride=D_MODEL // LANES), :] = h2f[:, j * LANES:(j + 1) * LANES]
    logits = jnp.dot(h2, wr_ref[...], preferred_element_type=F32)
    lane = lax.broadcasted_iota(jnp.int32, logits.shape, 1)
    logits = jnp.where(lane < N_EXPERTS, logits, NEG_BIG)
    lmax = jnp.max(logits, axis=-1, keepdims=True)
    ex = jnp.exp(logits - lmax)
    aff = ex / jnp.sum(ex, axis=-1, keepdims=True)
    afft_ref[0] = aff.T[:N_EXPERTS, :]


def _merge(of, ob, proj, y_b, x2d, gdn_norm, wa, wb, wo, norm_ffn, w_router, seq):
    n = x2d.shape[0]
    tm = min(512, seq)
    tpb = seq // tm
    wr = jnp.zeros((D_MODEL, LANES), BF16).at[:, :N_EXPERTS].set(w_router.astype(BF16))
    row = lambda w: pl.BlockSpec((tm, w), lambda i: (i, 0))
    full = lambda a: pl.BlockSpec(a.shape, lambda i: (0,) * a.ndim)
    gn = gdn_norm.reshape(1, GDN_DV)
    nf = norm_ffn.reshape(1, D_MODEL)
    in_specs = [
        row(GDN_W), row(GDN_W),
        pl.BlockSpec((tm, GDN_W), lambda i: (i, C_Z // GDN_W)),
        pl.BlockSpec((tm, D_MODEL), lambda i: (i, C_GATE // D_MODEL)),
        pl.BlockSpec((tm, D_MODEL), lambda i: (i, C_GATE // D_MODEL + 1)),
        row(ATT_GW),
        row(D_MODEL), full(gn), full(wa), full(wb), full(wo), full(nf), full(wr),
    ]
    return pl.pallas_call(
        _merge_kernel,
        grid=(n // tm,),
        in_specs=in_specs,
        out_specs=[
            row(D_MODEL),
            pl.BlockSpec((tm * (D_MODEL // LANES), LANES), lambda i: (i, 0)),
            pl.BlockSpec((1, N_EXPERTS, tm), lambda i: (i // tpb, 0, i % tpb)),
        ],
        out_shape=[
            jax.ShapeDtypeStruct((n, D_MODEL), F32),
            jax.ShapeDtypeStruct((n * (D_MODEL // LANES), LANES), F32),
            jax.ShapeDtypeStruct((n // seq, N_EXPERTS, seq), F32),
        ],
        compiler_params=_cparams(("parallel",)),
        name="merge",
    )(of, ob, proj, proj, proj, y_b, x2d, gn, wa, wb, wo, nf, wr)


def _prefix_tables(n_exp, nch):
    rows = n_exp * nch
    tri = (np.arange(LANES)[:, None] <= np.arange(LANES)[None, :]).astype(np.float32)
    ri = np.arange(rows)
    same = (ri[:, None] // nch) == (ri[None, :] // nch)
    btri = np.logical_and(same, (ri[None, :] % nch) < (ri[:, None] % nch)).astype(np.float32)
    return jnp.asarray(tri, BF16), jnp.asarray(btri, BF16)


def _route_kernel(aff_ref, tri_ref, btri_ref, cin_ref, start_ref, *, cap):
    x = aff_ref[0]
    n_exp, nch, _ = x.shape
    rows = n_exp * nch
    bits = lax.bitcast_convert_type(x, jnp.int32)

    def body(it, ans):
        cand = jnp.bitwise_or(ans, lax.shift_left(jnp.int32(1), 30 - it))
        cnt = jnp.sum(jnp.where(bits >= cand, 1.0, 0.0), axis=(1, 2), keepdims=True)
        return jnp.where(cnt >= cap, cand, ans)

    thr = lax.fori_loop(0, 31, body, jnp.zeros((n_exp, 1, 1), jnp.int32))
    gt = jnp.where(bits > thr, 1.0, 0.0)
    eq = jnp.where(bits == thr, 1.0, 0.0)
    need = cap - jnp.sum(gt, axis=(1, 2), keepdims=True)
    need_rows = jnp.broadcast_to(need, x.shape).reshape(rows, LANES)
    gt2 = gt.reshape(rows, LANES)
    eq2 = eq.reshape(rows, LANES)

    def prefix(mask2):
        cin = jnp.dot(mask2.astype(BF16), tri_ref[...], preferred_element_type=F32)
        tot = jnp.broadcast_to(cin[:, LANES - 1:LANES], cin.shape).astype(BF16)
        off = jnp.dot(btri_ref[...], tot, preferred_element_type=F32)
        return cin, off

    cin_eq, off_eq = prefix(eq2)
    eq_before = off_eq + cin_eq - eq2
    sel = jnp.maximum(gt2, jnp.where(eq_before < need_rows, eq2, 0.0))
    cin, off = prefix(sel)
    cin_ref[0] = cin
    start_ref[0] = off


def _route(afft, cap):
    b, n_exp, t = afft.shape
    nch = t // LANES
    rows = n_exp * nch
    tri, btri = _prefix_tables(n_exp, nch)
    aff4 = afft.reshape(b, n_exp, nch, LANES)
    kern = functools.partial(_route_kernel, cap=cap)
    return pl.pallas_call(
        kern,
        grid=(b,),
        in_specs=[
            pl.BlockSpec((1, n_exp, nch, LANES), lambda i: (i, 0, 0, 0)),
            pl.BlockSpec((LANES, LANES), lambda i: (0, 0)),
            pl.BlockSpec((rows, rows), lambda i: (0, 0)),
        ],
        out_specs=[pl.BlockSpec((1, rows, LANES), lambda i: (i, 0, 0))] * 2,
        out_shape=[jax.ShapeDtypeStruct((b, rows, LANES), F32)] * 2,
        compiler_params=_cparams(("parallel",)),
        name="route",
    )(aff4, tri, btri)


def _pad_rows(a, rows):
    if a.shape[0] == rows:
        return a
    return jnp.concatenate([a, jnp.zeros((rows - a.shape[0], a.shape[1]), a.dtype)], axis=0)


def _compact_kernel(cin_ref, start_ref, aff_ref, idxc_ref, gatec_ref, idxr_ref, *, cap):
    r = _pad_rows(cin_ref[0, 0], LANES)
    st = _pad_rows(start_ref[0, 0], LANES)
    en = st + r[:, LANES - 1:LANES]
    st_row = st.T[0:1, :]
    en_row = en.T[0:1, :]
    sig = lax.broadcasted_iota(jnp.int32, (cap, LANES), 0).astype(F32)
    lane = lax.broadcasted_iota(jnp.int32, (cap, LANES), 1).astype(F32)
    onehot = jnp.where(jnp.logical_and(st_row <= sig, sig < en_row), 1.0, 0.0)
    st_sel = jnp.sum(onehot * st_row, axis=1, keepdims=True)
    c_sel = jnp.sum(onehot * lane, axis=1, keepdims=True)
    oh16 = onehot.astype(BF16)
    r_sel = jnp.dot(oh16, r.astype(BF16), preferred_element_type=F32)
    before = sig[:, 0:1] - st_sel
    pos = jnp.sum(jnp.where(r_sel <= before, 1.0, 0.0), axis=1, keepdims=True)
    idx = c_sel * LANES + pos
    idxc_ref[0] = idx.astype(jnp.int32)
    a = _pad_rows(aff_ref[0, 0], LANES)
    a1 = a.astype(BF16)
    rem = a - a1.astype(F32)
    a2 = rem.astype(BF16)
    a3 = (rem - a2.astype(F32)).astype(BF16)
    a_sel = (jnp.dot(oh16, a1, preferred_element_type=F32) + jnp.dot(oh16, a2, preferred_element_type=F32)
             + jnp.dot(oh16, a3, preferred_element_type=F32))
    gatec_ref[0] = jnp.sum(jnp.where(lane == pos, a_sel, 0.0), axis=1, keepdims=True)
    idx_b = jnp.broadcast_to(idx, (cap, LANES))
    for kb in range(cap // LANES):
        blk = idx_b[kb * LANES:(kb + 1) * LANES, :].T
        idxr_ref[0, :, kb * LANES:(kb + 1) * LANES] = blk[0:1, :].astype(jnp.int32)


def _compact(cin, start, afft, cap):
    b, n_exp, t = afft.shape
    nch = t // LANES
    cin4 = cin.reshape(b, n_exp, nch, LANES)
    start4 = start.reshape(b, n_exp, nch, LANES)
    aff4 = afft.reshape(b, n_exp, nch, LANES)
    in_spec = pl.BlockSpec((1, 1, nch, LANES), lambda bi, e: (bi, e, 0, 0))
    kern = functools.partial(_compact_kernel, cap=cap)
    return pl.pallas_call(
        kern,
        grid=(b, n_exp),
        in_specs=[in_spec, in_spec, in_spec],
        out_specs=[
            pl.BlockSpec((1, cap, 1), lambda bi, e: (bi * n_exp + e, 0, 0)),
            pl.BlockSpec((1, cap, 1), lambda bi, e: (bi * n_exp + e, 0, 0)),
            pl.BlockSpec((1, 1, cap), lambda bi, e: (bi * n_exp + e, 0, 0)),
        ],
        out_shape=[
            jax.ShapeDtypeStruct((b * n_exp, cap, 1), jnp.int32),
            jax.ShapeDtypeStruct((b * n_exp, cap, 1), F32),
            jax.ShapeDtypeStruct((b * n_exp, 1, cap), jnp.int32),
        ],
        compiler_params=_cparams(("parallel", "parallel")),
        name="compact",
    )(cin4, start4, aff4)


MOE_TS = 256
MOE_GK = 8
TOKEN_ROWS = D_MODEL // LANES
GATHER_UNROLL = 8


def _ffn_kernel(idx_ref, h_ref, gate_ref, wg_ref, wu_ref, wd_ref, y_ref, xr_ref):
    tm = gate_ref.shape[1]

    def gather(i, carry):
        for u in range(GATHER_UNROLL):
            r = i * GATHER_UNROLL + u
            src = pl.multiple_of(idx_ref[0, 0, r] * TOKEN_ROWS, TOKEN_ROWS)
            dst = pl.multiple_of(r * TOKEN_ROWS, TOKEN_ROWS)
            xr_ref[pl.ds(dst, TOKEN_ROWS), :] = h_ref[pl.ds(src, TOKEN_ROWS), :]
        return carry

    lax.fori_loop(0, tm // GATHER_UNROLL, gather, 0)
    xe = jnp.concatenate([xr_ref[pl.ds(j, tm, stride=TOKEN_ROWS), :] for j in range(TOKEN_ROWS)],
                         axis=1).astype(BF16)
    a = jnp.dot(xe, wg_ref[0], preferred_element_type=F32)
    u = jnp.dot(xe, wu_ref[0], preferred_element_type=F32)
    act = (a * (1.0 / (1.0 + jnp.exp(-a))) * u).astype(BF16)
    y = jnp.dot(act, wd_ref[0], preferred_element_type=F32)
    y_ref[0] = (y * gate_ref[0]).astype(BF16)


def _ffn(h2r, idx_row, gate_col, w_gate, w_up, w_down, b, t, cap):
    n_exp = w_gate.shape[0]
    tm = min(512, cap)
    wspec = pl.BlockSpec((1, D_MODEL, EXPERT_FF), lambda bi, e, ct: (e, 0, 0))
    return pl.pallas_call(
        _ffn_kernel,
        grid=(b, n_exp, cap // tm),
        in_specs=[
            pl.BlockSpec((1, 1, tm), lambda bi, e, ct: (bi * n_exp + e, 0, ct), memory_space=pltpu.SMEM),
            pl.BlockSpec((t * TOKEN_ROWS, LANES), lambda bi, e, ct: (bi, 0), pipeline_mode=pl.Buffered(1)),
            pl.BlockSpec((1, tm, 1), lambda bi, e, ct: (bi * n_exp + e, ct, 0)),
            wspec, wspec,
            pl.BlockSpec((1, EXPERT_FF, D_MODEL), lambda bi, e, ct: (e, 0, 0)),
        ],
        out_specs=pl.BlockSpec((1, tm, D_MODEL), lambda bi, e, ct: (bi * n_exp + e, ct, 0)),
        out_shape=jax.ShapeDtypeStruct((b * n_exp, cap, D_MODEL), BF16),
        scratch_shapes=[pltpu.VMEM((tm * TOKEN_ROWS, LANES), F32)],
        compiler_params=_cparams(("arbitrary", "arbitrary", "arbitrary")),
        name="expert_ffn",
    )(idx_row, h2r, gate_col, w_gate, w_up, w_down)


def _combine_kernel(klo_ref, khi_ref, x1_ref, y_ref, idxr_ref, nfin_ref, o_ref, ystage_ref, istage_ref, acc_ref, *,
                    cap, final_norm):
    bi, ti = pl.program_id(0), pl.program_id(1)
    nts = pl.num_programs(1)
    ts2 = acc_ref.shape[0]
    n_exp = idxr_ref.shape[1]
    gk = MOE_GK * LANES

    @pl.when(jnp.logical_and(bi == 0, ti == 0))
    def _():
        ystage_ref[...] = jnp.zeros_like(ystage_ref)

    istage_ref[...] = jnp.full(istage_ref.shape, -1, jnp.int32)
    n = jnp.int32(0)
    for e in range(n_exp):
        tab = (bi * nts + ti) * n_exp + e

        def stage(k, n, e=e):
            s0 = pl.multiple_of(k * LANES, LANES)
            d0 = pl.multiple_of(n * LANES, LANES)
            ystage_ref[pl.ds(d0, LANES), :] = y_ref[0, pl.ds(e * cap + s0, LANES), :]
            istage_ref[:, pl.ds(d0, LANES)] = idxr_ref[0, e:e + 1, pl.ds(s0, LANES)]
            return n + 1

        n = lax.fori_loop(klo_ref[tab], khi_ref[tab] + 1, stage, n)
    tok = ti * ts2 + lax.broadcasted_iota(jnp.int32, (ts2, gk), 0)
    acc_ref[...] = jnp.zeros_like(acc_ref)

    def scatter(g, carry):
        k0 = pl.multiple_of(g * gk, gk)
        onehot = jnp.where(tok == istage_ref[:, pl.ds(k0, gk)], 1.0, 0.0).astype(BF16)
        acc_ref[...] += jnp.dot(onehot, ystage_ref[pl.ds(k0, gk), :], preferred_element_type=F32)
        return carry

    lax.fori_loop(0, (n + MOE_GK - 1) // MOE_GK, scatter, 0)
    out = x1_ref[...] + acc_ref[...]
    if final_norm:
        out = out * lax.rsqrt(jnp.mean(out * out, axis=-1, keepdims=True) + RMS_EPS) * nfin_ref[...]
    o_ref[...] = out


def _combine(x1, y, idx_row, start, norm_final, b, t, cap, final_norm):
    n_exp = N_EXPERTS
    ts2 = min(MOE_TS, t)
    nts = t // ts2
    nch = t // LANES
    st = start[:, :, 0].reshape(b, n_exp, nch)[:, :, ::ts2 // LANES].astype(jnp.int32)
    en = jnp.concatenate([st[:, :, 1:], jnp.full((b, n_exp, 1), cap, jnp.int32)], axis=2)
    klo = st // LANES
    khi = jnp.where(en > st, (en - 1) // LANES, klo - 1)
    klo_tab = jnp.transpose(klo, (0, 2, 1)).reshape(-1)
    khi_tab = jnp.transpose(khi, (0, 2, 1)).reshape(-1)
    y3 = y.reshape(b, n_exp * cap, D_MODEL)
    idxr3 = idx_row.reshape(b, n_exp, cap)
    max_chunks = n_exp * min(ts2 // LANES + 1, cap // LANES)
    stage_chunks = -(-max_chunks // MOE_GK) * MOE_GK
    kern = functools.partial(_combine_kernel, cap=cap, final_norm=final_norm)
    grid_spec = pltpu.PrefetchScalarGridSpec(
        num_scalar_prefetch=2,
        grid=(b, nts),
        in_specs=[
            pl.BlockSpec((ts2, D_MODEL), lambda bi, ti, lo, hi: (bi * nts + ti, 0)),
            pl.BlockSpec((1, n_exp * cap, D_MODEL), lambda bi, ti, lo, hi: (bi, 0, 0), pipeline_mode=pl.Buffered(1)),
            pl.BlockSpec((1, n_exp, cap), lambda bi, ti, lo, hi: (bi, 0, 0)),
            pl.BlockSpec((1, D_MODEL), lambda bi, ti, lo, hi: (0, 0)),
        ],
        out_specs=pl.BlockSpec((ts2, D_MODEL), lambda bi, ti, lo, hi: (bi * nts + ti, 0)),
        scratch_shapes=[pltpu.VMEM((stage_chunks * LANES, D_MODEL), BF16),
                        pltpu.VMEM((1, stage_chunks * LANES), jnp.int32),
                        pltpu.VMEM((ts2, D_MODEL), F32)],
    )
    return pl.pallas_call(
        kern,
        grid_spec=grid_spec,
        out_shape=jax.ShapeDtypeStruct((b * t, D_MODEL), F32),
        compiler_params=_cparams(("arbitrary", "arbitrary")),
        name="moe_combine",
    )(klo_tab, khi_tab, x1, y3, idxr3, norm_final.reshape(1, D_MODEL))


GDN_CHUNK = 64


def kernel(x, norm_mix, w_in, conv_w, a_log, dt_bias, gdn_norm, w_branch_a, w_branch_b, w_out, norm_ffn, w_router, w_expert_gate, w_expert_up, w_expert_down, norm_final):
    b, t, d = x.shape
    depth = w_in.shape[0]
    cap = EC_CAPACITY_FACTOR * t // N_EXPERTS
    rope_cos, rope_sin = _rope_tables(t)
    xc = x.reshape(b * t, d)
    for layer in range(depth):
        w_all, w_bd, w_att = _proj_weights(w_in[layer])
        nw = norm_mix[layer].reshape(1, d)
        proj, bd = _proj(xc, nw, w_all, w_bd)
        att_qkv = _att_proj(xc, nw, w_att, rope_cos, rope_sin, b, t)
        proj3 = proj.reshape(b, t, PROJ_W)
        qkv, aux, gcr = _gdn_prep(proj3, bd.reshape(b, t, LANES), conv_w[layer], a_log[layer], dt_bias[layer],
                                  GDN_CHUNK)
        o_f, o_b = _gdn_scan(qkv, aux, gcr, GDN_CHUNK)
        y_b = _attention(att_qkv, b, t)
        x1, h2r, afft = _merge(o_f.reshape(b * t, GDN_W), o_b.reshape(b * t, GDN_W), proj, y_b, xc,
                              gdn_norm[layer], w_branch_a[layer].astype(BF16), w_branch_b[layer].astype(BF16),
                              w_out[layer].astype(BF16), norm_ffn[layer], w_router[layer], t)
        cin, start = _route(afft, cap)
        idx_col, gate_col, idx_row = _compact(cin, start, afft, cap)
        y = _ffn(h2r, idx_row, gate_col, w_expert_gate[layer].astype(BF16),
                 w_expert_up[layer].astype(BF16), w_expert_down[layer].astype(BF16), b, t, cap)
        xc = _combine(x1, y, idx_row, start, norm_final, b, t, cap, final_norm=(layer == depth - 1))
    return xc.reshape(b, t, d)
```

```python
import functools

import numpy as np
import jax
import jax.numpy as jnp
from jax import lax
from jax.experimental import pallas as pl
from jax.experimental.pallas import tpu as pltpu

F32 = jnp.float32
BF16 = jnp.bfloat16

D_MODEL = 1024
GDN_HEADS = 4
GDN_DK = 128
GDN_DV = 128
CONV_WIDTH = 5
ATT_GROUPS = ((128, 1), (512, 4), (2048, 16))
ATT_HPG = 4
ATT_HEAD_DIM = 64
ATT_SPAN = 64
ROPE_THETA = 500000.0
ROPE_DIMS = ATT_HEAD_DIM // 4
N_EXPERTS = 16
EXPERT_FF = 1024
EC_CAPACITY_FACTOR = 2
RMS_EPS = 1e-6
NEG_BIG = -1e30

GDN_W = GDN_HEADS * GDN_DK
GDN_CONV_CH = 3 * GDN_W
ATT_GW = ATT_HPG * ATT_HEAD_DIM
ATT_W = len(ATT_GROUPS) * ATT_GW

C_QKVA = 0
C_Z = C_QKVA + GDN_CONV_CH
C_GATE = C_Z + GDN_W
PROJ_W = C_GATE + 2 * D_MODEL
PROJ_TN = 512
BD_W = 4 * GDN_HEADS

LANES = 128
VMEM_LIMIT = 56 * 1024 * 1024


def _cparams(sem):
    return pltpu.CompilerParams(dimension_semantics=sem, vmem_limit_bytes=VMEM_LIMIT)


def _att_proj_kernel(x_ref, nw_ref, wall_ref, wbd_ref, w_ref, rc_ref, rs_ref,
                     o_ref, bd_ref, o0_ref, o1_ref, o2_ref, scr_ref):
    x = x_ref[...]
    tm = x.shape[0]
    ms = jnp.mean(x * x, axis=-1, keepdims=True)
    h = (x * lax.rsqrt(ms + RMS_EPS) * nw_ref[...]).astype(BF16)
    for j in range(PROJ_W // PROJ_TN):
        cols = slice(j * PROJ_TN, (j + 1) * PROJ_TN)
        o_ref[:, cols] = jnp.dot(h, wall_ref[:, cols], preferred_element_type=F32).astype(BF16)
    bd_ref[...] = jnp.dot(h, wbd_ref[...], preferred_element_type=F32)
    reps = ATT_GW // LANES
    cos = jnp.concatenate([rc_ref[...]] * reps, axis=1)
    sin = jnp.concatenate([rs_ref[...]] * reps, axis=1)
    lane = lax.broadcasted_iota(jnp.int32, (tm, ATT_GW), 1) % ATT_HEAD_DIM
    half = ROPE_DIMS // 2
    outs = (o0_ref, o1_ref, o2_ref)
    for g, (_, r) in enumerate(ATT_GROUPS):
        for which in range(3):
            c0 = which * ATT_W + g * ATT_GW
            acc = jnp.dot(h, w_ref[:, c0:c0 + ATT_GW], preferred_element_type=F32)
            if which < 2:
                partner = jnp.where(lane < half, pltpu.roll(acc, ATT_GW - half, 1), pltpu.roll(acc, half, 1))
                acc = acc * cos + partner * sin
            cols = slice(which * ATT_GW, (which + 1) * ATT_GW)
            if r == 1:
                outs[g][0, 0, :, cols] = acc.astype(BF16)
            else:
                for hf in range(reps):
                    scr_ref[hf] = acc[:, hf * LANES:(hf + 1) * LANES]
                for s in range(r):
                    sub = [scr_ref[hf, pl.ds(s, tm // r, stride=r), :] for hf in range(reps)]
                    outs[g][0, s, :, cols] = jnp.concatenate(sub, axis=1).astype(BF16)


def _att_proj(x2d, norm_w, w_all, w_bd, w_att, rope_cos, rope_sin, b, seq):
    n = x2d.shape[0]
    tm = min(512, seq)
    tpb = seq // tm
    out_specs = [pl.BlockSpec((tm, PROJ_W), lambda i: (i, 0)), pl.BlockSpec((tm, LANES), lambda i: (i, 0))]
    out_shape = [jax.ShapeDtypeStruct((n, PROJ_W), BF16), jax.ShapeDtypeStruct((n, LANES), F32)]
    for _, r in ATT_GROUPS:
        out_specs.append(pl.BlockSpec((1, r, tm // r, 3 * ATT_GW), lambda i: (i // tpb, 0, i % tpb, 0)))
        out_shape.append(jax.ShapeDtypeStruct((b, r, seq // r, 3 * ATT_GW), BF16))
    return pl.pallas_call(
        _att_proj_kernel,
        grid=(n // tm,),
        in_specs=[
            pl.BlockSpec((tm, D_MODEL), lambda i: (i, 0)),
            pl.BlockSpec((1, D_MODEL), lambda i: (0, 0)),
            pl.BlockSpec((D_MODEL, PROJ_W), lambda i: (0, 0), pipeline_mode=pl.Buffered(1)),
            pl.BlockSpec((D_MODEL, LANES), lambda i: (0, 0)),
            pl.BlockSpec((D_MODEL, 3 * ATT_W), lambda i: (0, 0), pipeline_mode=pl.Buffered(1)),
            pl.BlockSpec((tm, LANES), lambda i: (i % tpb, 0)),
            pl.BlockSpec((tm, LANES), lambda i: (i % tpb, 0)),
        ],
        out_specs=out_specs,
        out_shape=out_shape,
        scratch_shapes=[pltpu.VMEM((ATT_GW // LANES, tm, LANES), F32)],
        compiler_params=_cparams(("parallel",)),
        name="in_proj",
    )(x2d, norm_w, w_all, w_bd, w_att, rope_cos, rope_sin)


def _rope_tables(seq):
    half = ROPE_DIMS // 2
    inv_freq = jnp.power(ROPE_THETA, -jnp.arange(half, dtype=F32) * 2.0 / ROPE_DIMS)
    ang = jnp.arange(seq, dtype=F32)[:, None] * inv_freq[None, :]
    cos, sin = jnp.cos(ang), jnp.sin(ang)
    rest = ATT_HEAD_DIM - ROPE_DIMS
    cos_h = jnp.concatenate([cos, cos, jnp.ones((seq, rest), F32)], axis=1)
    sin_h = jnp.concatenate([-sin, sin, jnp.zeros((seq, rest), F32)], axis=1)
    reps = LANES // ATT_HEAD_DIM
    return jnp.tile(cos_h, (1, reps)), jnp.tile(sin_h, (1, reps))


def _proj_weights(w_in):
    off_z = GDN_CONV_CH
    off_beta = off_z + GDN_W
    off_att = off_beta + 4 * GDN_HEADS
    off_gate = off_att + 3 * ATT_W
    qkva = w_in[:, :off_z]
    z = w_in[:, off_z:off_beta]
    bd = w_in[:, off_beta:off_att]
    att = w_in[:, off_att:off_gate]
    gates = w_in[:, off_gate:]
    att = jnp.concatenate([att[:, :ATT_W] * (ATT_HEAD_DIM ** -0.5), att[:, ATT_W:]], axis=1)
    bd = jnp.concatenate([bd, jnp.zeros((w_in.shape[0], LANES - BD_W), w_in.dtype)], axis=1)
    return jnp.concatenate([qkva, z, gates], axis=1).astype(BF16), bd.astype(BF16), att.astype(BF16)


AUX_BETA = 0
AUX_GC = 8
AUX_EG = 16
AUX_EK = 24
AUX_GE = 32
HALO = 16


def _gdn_prep_kernel(xp_ref, xc_ref, xn_ref, bd_ref, cw_ref, alog_ref, dtb_ref,
                     qkv_ref, aux_ref, gcr_ref, xf_ref, *, chunk):
    i = pl.program_id(1)
    nblk = pl.num_programs(1)
    tp = xc_ref.shape[1]
    prev = jnp.where(i > 0, xp_ref[0].astype(F32), 0.0)
    nxt = jnp.where(i < nblk - 1, xn_ref[0].astype(F32), 0.0)
    xf_ref[0:HALO, :] = prev
    xf_ref[HALO:HALO + tp, :] = xc_ref[0].astype(F32)
    xf_ref[HALO + tp:, :] = nxt
    pad = (CONV_WIDTH - 1) // 2
    for hb in range(3 * GDN_HEADS):
        sl = slice(hb * GDN_DK, (hb + 1) * GDN_DK)
        conv = None
        for j in range(CONV_WIDTH):
            term = xf_ref[pl.ds(HALO - pad + j, tp), sl] * cw_ref[j:j + 1, sl]
            conv = term if conv is None else conv + term
        uh = conv * (1.0 / (1.0 + jnp.exp(-conv)))
        if hb < 2 * GDN_HEADS:
            scale = lax.rsqrt(jnp.sum(uh * uh, axis=-1, keepdims=True) + 1e-6)
            if hb < GDN_HEADS:
                scale = scale * (GDN_DK ** -0.5)
            uh = uh * scale
        qkv_ref[0, :, sl] = uh.astype(BF16)

    nh2 = 2 * GDN_HEADS
    bd = bd_ref[0]
    lane = lax.broadcasted_iota(jnp.int32, bd.shape, 1)
    row = lax.broadcasted_iota(jnp.int32, bd.shape, 0) % chunk
    beta = 1.0 / (1.0 + jnp.exp(-bd))
    sp_in = bd + dtb_ref[...]
    softplus = jnp.maximum(sp_in, 0.0) + jnp.log(1.0 + jnp.exp(-jnp.abs(sp_in)))
    g = -jnp.exp(alog_ref[...]) * softplus
    g = jnp.where(jnp.logical_and(lane >= nh2, lane < 2 * nh2), g, 0.0)
    pre = g
    suf = g
    s = 1
    while s < chunk:
        pre = pre + jnp.where(row >= s, pltpu.roll(pre, s, 0), 0.0)
        suf = suf + jnp.where(row < chunk - s, pltpu.roll(suf, tp - s, 0), 0.0)
        s *= 2
    total = pre + suf - g
    is_fwd = lane < nh2 + GDN_HEADS
    gc = jnp.where(is_fwd, pre, suf)
    eg = jnp.exp(gc)
    ek = jnp.exp(total - gc)
    ge = jnp.exp(total)
    aux = jnp.where(lane < nh2, beta, 0.0)
    aux = aux + gc
    aux = aux + jnp.where(jnp.logical_and(lane >= AUX_EG, lane < AUX_EG + nh2), pltpu.roll(eg, AUX_EG - AUX_GC, 1), 0.0)
    aux = aux + jnp.where(jnp.logical_and(lane >= AUX_EK, lane < AUX_EK + nh2), pltpu.roll(ek, AUX_EK - AUX_GC, 1), 0.0)
    aux = aux + jnp.where(jnp.logical_and(lane >= AUX_GE, lane < AUX_GE + nh2), pltpu.roll(ge, AUX_GE - AUX_GC, 1), 0.0)
    aux_ref[0] = aux
    gct = gc.T
    for c in range(tp // chunk):
        g = gct[AUX_GC:AUX_GC + nh2, c * chunk:(c + 1) * chunk]
        gcr_ref[0, c] = jnp.concatenate([g, g], axis=1)


def _gdn_prep(proj3, bd3, conv_w, a_log, dt_bias, chunk):
    b, t, _ = proj3.shape
    tp = min(256, t)
    nblk = t // tp
    nh2 = 2 * GDN_HEADS
    alog_row = jnp.zeros((1, LANES), F32).at[0, nh2:2 * nh2].set(a_log.reshape(-1))
    dtb_row = jnp.zeros((1, LANES), F32).at[0, nh2:2 * nh2].set(dt_bias.reshape(-1))
    hb = tp // HALO
    kern = functools.partial(_gdn_prep_kernel, chunk=chunk)
    return pl.pallas_call(
        kern,
        grid=(b, nblk),
        in_specs=[
            pl.BlockSpec((1, HALO, GDN_CONV_CH), lambda bi, i: (bi, jnp.maximum(i * hb - 1, 0), 0)),
            pl.BlockSpec((1, tp, GDN_CONV_CH), lambda bi, i: (bi, i, 0)),
            pl.BlockSpec((1, HALO, GDN_CONV_CH), lambda bi, i: (bi, jnp.minimum((i + 1) * hb, t // HALO - 1), 0)),
            pl.BlockSpec((1, tp, LANES), lambda bi, i: (bi, i, 0)),
            pl.BlockSpec((CONV_WIDTH, GDN_CONV_CH), lambda bi, i: (0, 0)),
            pl.BlockSpec((1, LANES), lambda bi, i: (0, 0)),
            pl.BlockSpec((1, LANES), lambda bi, i: (0, 0)),
        ],
        out_specs=[
            pl.BlockSpec((1, tp, GDN_CONV_CH), lambda bi, i: (bi, i, 0)),
            pl.BlockSpec((1, tp, LANES), lambda bi, i: (bi, i, 0)),
            pl.BlockSpec((1, tp // chunk, nh2, 2 * chunk), lambda bi, i: (bi, i, 0, 0)),
        ],
        out_shape=[
            jax.ShapeDtypeStruct((b, t, GDN_CONV_CH), BF16),
            jax.ShapeDtypeStruct((b, t, LANES), F32),
            jax.ShapeDtypeStruct((b, t // chunk, nh2, 2 * chunk), F32),
        ],
        scratch_shapes=[pltpu.VMEM((tp + 2 * HALO, GDN_CONV_CH), F32)],
        compiler_params=_cparams(("parallel", "parallel")),
        name="gdn_prep",
    )(proj3, proj3, proj3, bd3, conv_w, alog_row, dtb_row)


GDN_CHUNKS_PER_ITER = 4
_DN_T = (((1,), (1,)), ((), ()))
_DN_L = (((0,), (0,)), ((), ()))


def _block_diag(pair, is_b):
    zero = jnp.zeros_like(pair)
    return jnp.concatenate([jnp.where(is_b, zero, pair), jnp.where(is_b, pair, zero)], axis=0)


def _gdn_intra_step(pairs, chunk):
    assert 2 * chunk == LANES
    ri = lax.broadcasted_iota(jnp.int32, (chunk, LANES), 0)
    lane = lax.broadcasted_iota(jnp.int32, (chunk, LANES), 1)
    is_b = lane >= chunk
    ci = jnp.where(is_b, lane - chunk, lane)
    below = jnp.where(is_b, ci - ri, ri - ci)
    incl = below >= 0
    strict = below > 0
    eye = jnp.where(ri == ci, 1.0, 0.0)
    for pr in pairs:
        for un in (pr["f"], pr["b"]):
            col, aux = un["col"], un["aux"]
            beta = aux[:, AUX_BETA + col:AUX_BETA + col + 1]
            eg = aux[:, AUX_EG + col:AUX_EG + col + 1]
            ek = aux[:, AUX_EK + col:AUX_EK + col + 1]
            un["gc"] = aux[:, AUX_GC + col:AUX_GC + col + 1]
            un["ge"] = aux[0:1, AUX_GE + col:AUX_GE + col + 1]
            un["kb"] = un["k"] * beta
            un["rhs"] = jnp.concatenate([un["v"] * beta, un["kb"] * eg], axis=1).astype(BF16)
            un["qd"] = un["q"] * eg
            un["kd"] = (un["k"] * ek).astype(BF16)
        f, b = pr["f"], pr["b"]
        gc = jnp.where(is_b, b["gc"], f["gc"])
        gcr = jnp.where(is_b[0:1], b["gcr"], f["gcr"])
        pr["decay"] = jnp.where(incl, jnp.exp(jnp.minimum(gc - gcr, 0.0)), 0.0)
        zk = jnp.zeros_like(f["k"])
        pr["kbq"] = jnp.concatenate([jnp.concatenate([f["kb"], b["kb"]], axis=1),
                                     jnp.concatenate([f["q"], b["q"]], axis=1)], axis=0).astype(BF16)
        pr["kt"] = jnp.concatenate([jnp.concatenate([f["k"], zk], axis=1),
                                    jnp.concatenate([zk, b["k"]], axis=1)], axis=0).astype(BF16)
        pr["rhs2"] = jnp.concatenate([f["rhs"], b["rhs"]], axis=0)
    for pr in pairs:
        pr["kkqk"] = lax.dot_general(pr["kbq"], pr["kt"], _DN_T, preferred_element_type=F32)
    for pr in pairs:
        lmat = jnp.where(strict, pr["kkqk"][:chunk] * pr["decay"], 0.0)
        pr["amat"] = _block_diag(pr["kkqk"][chunk:] * pr["decay"], is_b).astype(BF16)
        pr["pw"] = lmat
        pr["tinv"] = eye - lmat
    n_sq = max(chunk.bit_length() - 2, 0)
    for i in range(n_sq + 1):
        for pr in pairs:
            pw_bd = _block_diag(pr["pw"], is_b).astype(BF16)
            if i == 0:
                lhs = pr["pw"]
            elif i < n_sq:
                lhs = jnp.concatenate([pr["pw"], pr["tinv"]], axis=0)
            else:
                lhs = pr["tinv"]
            pr["prod"] = jnp.dot(lhs.astype(BF16), pw_bd, preferred_element_type=F32)
        for pr in pairs:
            if i == 0:
                pr["pw"] = pr["prod"]
            elif i < n_sq:
                pr["tinv"] = pr["tinv"] + pr["prod"][chunk:]
                pr["pw"] = pr["prod"][:chunk]
            else:
                pr["tinv"] = pr["tinv"] + pr["prod"]
    for pr in pairs:
        uw2 = jnp.dot(_block_diag(pr["tinv"], is_b).astype(BF16), pr["rhs2"], preferred_element_type=F32)
        pr["f"]["uw"], pr["b"]["uw"] = uw2[:chunk], uw2[chunk:]


def _gdn_state_step(pairs, chunk):
    units = [un for pr in pairs for un in (pr["f"], pr["b"])]
    for un in units:
        lhs = jnp.concatenate([un["uw"][:, GDN_DV:], un["qd"]], axis=0).astype(BF16)
        un["ws"] = jnp.dot(lhs, un["s"].astype(BF16), preferred_element_type=F32)
    for un in units:
        un["v_new"] = (un["uw"][:, :GDN_DV] - un["ws"][:chunk]).astype(BF16)
    for pr in pairs:
        f, b = pr["f"], pr["b"]
        av = jnp.dot(pr["amat"], jnp.concatenate([f["v_new"], b["v_new"]], axis=0), preferred_element_type=F32)
        f["o"] = f["ws"][chunk:] + av[:chunk]
        b["o"] = b["ws"][chunk:] + av[chunk:]
    for un in units:
        un["s_new"] = un["s"] * un["ge"] + lax.dot_general(un["kd"], un["v_new"], _DN_L, preferred_element_type=F32)


def _gdn_scan_kernel(qkvf_ref, qkvb_ref, auxf_ref, auxb_ref, gcrf_ref, gcrb_ref,
                     of_ref, ob_ref, s_ref, *, chunk):
    @pl.when(pl.program_id(1) == 0)
    def _():
        s_ref[...] = jnp.zeros_like(s_ref)

    nc = qkvf_ref.shape[1] // chunk
    dirs = ((qkvf_ref, auxf_ref, gcrf_ref, of_ref), (qkvb_ref, auxb_ref, gcrb_ref, ob_ref))

    per_iter = GDN_CHUNKS_PER_ITER if nc % GDN_CHUNKS_PER_ITER == 0 else 1

    def body(it, carry):
        groups = []
        for sub in range(per_iter):
            c = it * per_iter + sub
            by_dir = []
            for d, (qkv_ref, aux_ref, gcr_ref, o_ref) in enumerate(dirs):
                cc = c if d == 0 else nc - 1 - c
                r0 = pl.multiple_of(cc * chunk, chunk)
                aux = aux_ref[0, pl.ds(r0, chunk), :]
                gcr_all = gcr_ref[0, cc]
                units = []
                for h in range(GDN_HEADS):
                    si = d * GDN_HEADS + h
                    units.append(dict(
                        q=qkv_ref[0, pl.ds(r0, chunk), h * GDN_DK:(h + 1) * GDN_DK],
                        k=qkv_ref[0, pl.ds(r0, chunk), GDN_W + h * GDN_DK:GDN_W + (h + 1) * GDN_DK],
                        v=qkv_ref[0, pl.ds(r0, chunk), 2 * GDN_W + h * GDN_DV:2 * GDN_W + (h + 1) * GDN_DV],
                        aux=aux, gcr=gcr_all[si:si + 1, :], col=si, d=d, h=h, r0=r0, o_ref=o_ref))
                by_dir.append(units)
            groups.append([dict(f=uf, b=ub) for uf, ub in zip(*by_dir)])
        _gdn_intra_step([pr for pairs in groups for pr in pairs], chunk)
        state = [s_ref[si] for si in range(2 * GDN_HEADS)]
        for pairs in groups:
            units = [un for pr in pairs for un in (pr["f"], pr["b"])]
            for un in units:
                un["s"] = state[un["col"]]
            _gdn_state_step(pairs, chunk)
            for un in units:
                state[un["col"]] = un["s_new"]
                un["o_ref"][0, pl.ds(un["r0"], chunk), un["h"] * GDN_DV:(un["h"] + 1) * GDN_DV] = un["o"]
        for si in range(2 * GDN_HEADS):
            s_ref[si] = state[si]
        return carry

    lax.fori_loop(0, nc // per_iter, body, 0)


def _gdn_scan(qkv, aux, gcr, chunk):
    b, t, _ = qkv.shape
    tb = min(512, t)
    nb = t // tb
    nh2 = 2 * GDN_HEADS
    kern = functools.partial(_gdn_scan_kernel, chunk=chunk)
    fwd = lambda bi, i: (bi, i, 0)
    bwd = lambda bi, i: (bi, nb - 1 - i, 0)
    return pl.pallas_call(
        kern,
        grid=(b, nb),
        in_specs=[
            pl.BlockSpec((1, tb, GDN_CONV_CH), fwd),
            pl.BlockSpec((1, tb, GDN_CONV_CH), bwd),
            pl.BlockSpec((1, tb, LANES), fwd),
            pl.BlockSpec((1, tb, LANES), bwd),
            pl.BlockSpec((1, tb // chunk, nh2, 2 * chunk), lambda bi, i: (bi, i, 0, 0)),
            pl.BlockSpec((1, tb // chunk, nh2, 2 * chunk), lambda bi, i: (bi, nb - 1 - i, 0, 0)),
        ],
        out_specs=[pl.BlockSpec((1, tb, GDN_W), fwd), pl.BlockSpec((1, tb, GDN_W), bwd)],
        out_shape=[jax.ShapeDtypeStruct((b, t, GDN_W), F32)] * 2,
        scratch_shapes=[pltpu.VMEM((nh2, GDN_DK, GDN_DV), F32)],
        compiler_params=_cparams(("parallel", "arbitrary")),
        name="gdn_scan",
    )(qkv, qkv, aux, aux, gcr, gcr)


ATT_QT = 128


ATT_BLK = 2048


ATT_TILES_PER_ITER = 2


def _attn_tiles(tiles):
    head_of_lane = lax.broadcasted_iota(jnp.int32, (ATT_QT, ATT_GW), 1) // ATT_HEAD_DIM
    in_head = [head_of_lane == h for h in range(ATT_HPG)]
    for tl in tiles:
        q = tl["q"]
        q4 = jnp.concatenate([jnp.where(m, q, jnp.zeros_like(q)) for m in in_head], axis=0)
        tl["s"] = lax.dot_general(q4, tl["kt"], _DN_T, preferred_element_type=F32)
    for tl in tiles:
        nk = tl["s"].shape[1]
        s = jnp.where(tl["valid"][None], tl["s"].reshape(ATT_HPG, ATT_QT, nk), NEG_BIG)
        mx = jnp.max(s, axis=-1, keepdims=True)
        p = jnp.exp(s - mx)
        tl["l"] = jnp.sum(p, axis=-1, keepdims=True)
        tl["mx"] = mx
        tl["p"] = p.reshape(ATT_HPG * ATT_QT, nk).astype(BF16)
    for tl in tiles:
        tl["pv"] = jnp.dot(tl["p"], tl["vt"], preferred_element_type=F32)
    for tl in tiles:
        o = jnp.zeros((ATT_QT, ATT_GW), F32)
        lse = jnp.zeros((ATT_QT, ATT_GW), F32)
        for h in range(ATT_HPG):
            rows = slice(h * ATT_QT, (h + 1) * ATT_QT)
            o = jnp.where(in_head[h], tl["pv"][rows] / tl["l"][h], o)
            lse = jnp.where(in_head[h], tl["mx"][h] + jnp.log(tl["l"][h]), lse)
        tl["o"], tl["lse"] = o, lse


def _attn_kernel(*refs, seq):
    n_g = len(ATT_GROUPS)
    ins = refs[:7 * n_g]
    y_ref = refs[7 * n_g]
    kv_scr = refs[7 * n_g + 1:7 * n_g + 1 + 2 * n_g]
    o_scr, l_scr = refs[7 * n_g + 1 + 2 * n_g:]
    i = pl.program_id(1)
    n_half = ATT_GW // LANES
    nk = ATT_QT + 2 * ATT_SPAN
    qi = lax.broadcasted_iota(jnp.int32, (ATT_QT, nk), 0)
    ki = lax.broadcasted_iota(jnp.int32, (ATT_QT, nk), 1)
    rel = ki - qi
    band = jnp.logical_and(rel >= 0, rel <= 2 * ATT_SPAN)
    for g, (_, r) in enumerate(ATT_GROUPS):
        q_ref, kp_ref, kc_ref, kn_ref, vp_ref, vc_ref, vn_ref = ins[7 * g:7 * g + 7]
        kf_ref, vf_ref = kv_scr[2 * g:2 * g + 2]
        rows = ATT_BLK // r
        m_total = seq // r
        for full_ref, parts in ((kf_ref, (kp_ref, kc_ref, kn_ref)), (vf_ref, (vp_ref, vc_ref, vn_ref))):
            full_ref[:, 0:ATT_SPAN, :] = parts[0][0]
            full_ref[:, ATT_SPAN:ATT_SPAN + rows, :] = parts[1][0]
            full_ref[:, ATT_SPAN + rows:, :] = parts[2][0]
        tiles = rows // ATT_QT

        assert (r * tiles) % ATT_TILES_PER_ITER == 0

        def tile_group(it, carry, r=r, g=g, rows=rows, tiles=tiles, m_total=m_total,
                       q_ref=q_ref, kf_ref=kf_ref, vf_ref=vf_ref):
            group = []
            for u in range(ATT_TILES_PER_ITER):
                idx = it * ATT_TILES_PER_ITER + u
                s = idx // tiles
                r0 = pl.multiple_of((idx % tiles) * ATT_QT, ATT_QT)
                mk = i * rows + r0 - ATT_SPAN + ki
                group.append(dict(
                    res=s, r0=r0, q=q_ref[0, s, pl.ds(r0, ATT_QT), :],
                    kt=kf_ref[s, pl.ds(r0, nk), :], vt=vf_ref[s, pl.ds(r0, nk), :],
                    valid=jnp.logical_and(band, jnp.logical_and(mk >= 0, mk < m_total))))
            _attn_tiles(group)
            for tl in group:
                for hf in range(n_half):
                    lanes = slice(hf * LANES, (hf + 1) * LANES)
                    if r == 1:
                        dst = pl.ds(tl["r0"], ATT_QT)
                    else:
                        dst = pl.ds(tl["r0"] * r + tl["res"], ATT_QT, stride=r)
                    o_scr[g * n_half + hf, dst, :] = tl["o"][:, lanes]
                    l_scr[g * n_half + hf, dst, :] = tl["lse"][:, lanes]
            return carry

        lax.fori_loop(0, r * tiles // ATT_TILES_PER_ITER, tile_group, 0)

    cb = 256

    def comb(c, carry):
        r0 = pl.multiple_of(c * cb, cb)
        halves = []
        for hf in range(n_half):
            ls = [l_scr[g * n_half + hf, pl.ds(r0, cb), :] for g in range(n_g)]
            lm = functools.reduce(jnp.maximum, ls)
            es = [jnp.exp(lv - lm) for lv in ls]
            num = sum(es[g] * o_scr[g * n_half + hf, pl.ds(r0, cb), :] for g in range(n_g))
            halves.append(num / sum(es))
        y_ref[pl.ds(r0, cb), :] = jnp.concatenate(halves, axis=1).astype(BF16)
        return carry

    lax.fori_loop(0, ATT_BLK // cb, comb, 0)


def _attention(att_qkv, b, seq):
    assert seq % ATT_BLK == 0
    nblk = seq // ATT_BLK
    in_specs, scratch, args = [], [], []
    for (_, r), arr in zip(ATT_GROUPS, att_qkv):
        rows = ATT_BLK // r
        assert rows % ATT_QT == 0
        hb = rows // ATT_SPAN
        last = seq // r // ATT_SPAN - 1

        def center(c, r=r, rows=rows):
            return pl.BlockSpec((1, r, rows, ATT_GW), lambda bi, i: (bi, 0, i, c))

        def prev(c, r=r, hb=hb):
            return pl.BlockSpec((1, r, ATT_SPAN, ATT_GW), lambda bi, i: (bi, 0, jnp.maximum(i * hb - 1, 0), c))

        def nxt(c, r=r, hb=hb, last=last):
            return pl.BlockSpec((1, r, ATT_SPAN, ATT_GW), lambda bi, i: (bi, 0, jnp.minimum((i + 1) * hb, last), c))

        in_specs += [center(0), prev(1), center(1), nxt(1), prev(2), center(2), nxt(2)]
        args += [arr] * 7
        scratch += [pltpu.VMEM((r, rows + 2 * ATT_SPAN, ATT_GW), BF16)] * 2
    n_g = len(ATT_GROUPS)
    scratch += [pltpu.VMEM((n_g * (ATT_GW // LANES), ATT_BLK, LANES), F32)] * 2
    kern = functools.partial(_attn_kernel, seq=seq)
    return pl.pallas_call(
        kern,
        grid=(b, nblk),
        in_specs=in_specs,
        out_specs=pl.BlockSpec((ATT_BLK, ATT_GW), lambda bi, i: (bi * nblk + i, 0)),
        out_shape=jax.ShapeDtypeStruct((b * seq, ATT_GW), BF16),
        scratch_shapes=scratch,
        compiler_params=_cparams(("parallel", "parallel")),
        name="attention",
    )(*args)


def _merge_kernel(of_ref, ob_ref, z_ref, ga_ref, gb_ref, yb_ref,
                  x_ref, gn_ref, wa_ref, wb_ref, wo_ref, nf_ref, wr_ref,
                  x1_ref, h2r_ref, afft_ref):
    o = of_ref[...] + ob_ref[...]
    z = z_ref[...].astype(F32)
    ya_parts = []
    for h in range(GDN_HEADS):
        sl = slice(h * GDN_DV, (h + 1) * GDN_DV)
        oh = o[:, sl]
        oh = oh * lax.rsqrt(jnp.mean(oh * oh, axis=-1, keepdims=True) + RMS_EPS) * gn_ref[...]
        zh = z[:, sl]
        ya_parts.append(oh * (zh * (1.0 / (1.0 + jnp.exp(-zh)))))
    y_a = jnp.concatenate(ya_parts, axis=1).astype(BF16)
    y_b = yb_ref[...]
    ga = 1.0 / (1.0 + jnp.exp(-ga_ref[...].astype(F32)))
    gb = 1.0 / (1.0 + jnp.exp(-gb_ref[...].astype(F32)))
    mixed = (ga * jnp.dot(y_a, wa_ref[...], preferred_element_type=F32)
             + gb * jnp.dot(y_b, wb_ref[...], preferred_element_type=F32))
    x1 = x_ref[...] + jnp.dot(mixed.astype(BF16), wo_ref[...], preferred_element_type=F32)
    x1_ref[...] = x1
    h2f = x1 * lax.rsqrt(jnp.mean(x1 * x1, axis=-1, keepdims=True) + RMS_EPS) * nf_ref[...]
    h2 = h2f.astype(BF16)
    tm = h2f.shape[0]
    for j in range(D_MODEL // LANES):
        h2r_ref[pl.ds(j, tm, stride=D_MODEL // LANES), :] = h2f[:, j * LANES:(j + 1) * LANES]
    logits = jnp.dot(h2, wr_ref[...], preferred_element_type=F32)
    lane = lax.broadcasted_iota(jnp.int32, logits.shape, 1)
    logits = jnp.where(lane < N_EXPERTS, logits, NEG_BIG)
    lmax = jnp.max(logits, axis=-1, keepdims=True)
    ex = jnp.exp(logits - lmax)
    aff = ex / jnp.sum(ex, axis=-1, keepdims=True)
    afft_ref[0] = aff.T[:N_EXPERTS, :]


def _merge(of, ob, proj, y_b, x2d, gdn_norm, wa, wb, wo, norm_ffn, w_router, seq):
    n = x2d.shape[0]
    tm = min(512, seq)
    tpb = seq // tm
    wr = jnp.zeros((D_MODEL, LANES), BF16).at[:, :N_EXPERTS].set(w_router.astype(BF16))
    row = lambda w: pl.BlockSpec((tm, w), lambda i: (i, 0))
    full = lambda a: pl.BlockSpec(a.shape, lambda i: (0,) * a.ndim)
    gn = gdn_norm.reshape(1, GDN_DV)
    nf = norm_ffn.reshape(1, D_MODEL)
    in_specs = [
        row(GDN_W), row(GDN_W),
        pl.BlockSpec((tm, GDN_W), lambda i: (i, C_Z // GDN_W)),
        pl.BlockSpec((tm, D_MODEL), lambda i: (i, C_GATE // D_MODEL)),
        pl.BlockSpec((tm, D_MODEL), lambda i: (i, C_GATE // D_MODEL + 1)),
        row(ATT_GW),
        row(D_MODEL), full(gn), full(wa), full(wb), full(wo), full(nf), full(wr),
    ]
    return pl.pallas_call(
        _merge_kernel,
        grid=(n // tm,),
        in_specs=in_specs,
        out_specs=[
            row(D_MODEL),
            pl.BlockSpec((tm * (D_MODEL // LANES), LANES), lambda i: (i, 0)),
            pl.BlockSpec((1, N_EXPERTS, tm), lambda i: (i // tpb, 0, i % tpb)),
        ],
        out_shape=[
            jax.ShapeDtypeStruct((n, D_MODEL), F32),
            jax.ShapeDtypeStruct((n * (D_MODEL // LANES), LANES), F32),
            jax.ShapeDtypeStruct((n // seq, N_EXPERTS, seq), F32),
        ],
        compiler_params=_cparams(("parallel",)),
        name="merge",
    )(of, ob, proj, proj, proj, y_b, x2d, gn, wa, wb, wo, nf, wr)


def _prefix_tables(n_exp, nch):
    rows = n_exp * nch
    tri = (np.arange(LANES)[:, None] <= np.arange(LANES)[None, :]).astype(np.float32)
    ri = np.arange(rows)
    same = (ri[:, None] // nch) == (ri[None, :] // nch)
    btri = np.logical_and(same, (ri[None, :] % nch) < (ri[:, None] % nch)).astype(np.float32)
    return jnp.asarray(tri, BF16), jnp.asarray(btri, BF16)


def _route_kernel(aff_ref, tri_ref, btri_ref, cin_ref, start_ref, *, cap):
    x = aff_ref[0]
    n_exp, nch, _ = x.shape
    rows = n_exp * nch
    bits = lax.bitcast_convert_type(x, jnp.int32)

    def body(it, ans):
        cand = jnp.bitwise_or(ans, lax.shift_left(jnp.int32(1), 30 - it))
        cnt = jnp.sum(jnp.where(bits >= cand, 1.0, 0.0), axis=(1, 2), keepdims=True)
        return jnp.where(cnt >= cap, cand, ans)

    thr = lax.fori_loop(0, 31, body, jnp.zeros((n_exp, 1, 1), jnp.int32))
    gt = jnp.where(bits > thr, 1.0, 0.0)
    eq = jnp.where(bits == thr, 1.0, 0.0)
    need = cap - jnp.sum(gt, axis=(1, 2), keepdims=True)
    need_rows = jnp.broadcast_to(need, x.shape).reshape(rows, LANES)
    gt2 = gt.reshape(rows, LANES)
    eq2 = eq.reshape(rows, LANES)

    def prefix(mask2):
        cin = jnp.dot(mask2.astype(BF16), tri_ref[...], preferred_element_type=F32)
        tot = jnp.broadcast_to(cin[:, LANES - 1:LANES], cin.shape).astype(BF16)
        off = jnp.dot(btri_ref[...], tot, preferred_element_type=F32)
        return cin, off

    cin_eq, off_eq = prefix(eq2)
    eq_before = off_eq + cin_eq - eq2
    sel = jnp.maximum(gt2, jnp.where(eq_before < need_rows, eq2, 0.0))
    cin, off = prefix(sel)
    cin_ref[0] = cin
    start_ref[0] = off


def _route(afft, cap):
    b, n_exp, t = afft.shape
    nch = t // LANES
    rows = n_exp * nch
    tri, btri = _prefix_tables(n_exp, nch)
    aff4 = afft.reshape(b, n_exp, nch, LANES)
    kern = functools.partial(_route_kernel, cap=cap)
    return pl.pallas_call(
        kern,
        grid=(b,),
        in_specs=[
            pl.BlockSpec((1, n_exp, nch, LANES), lambda i: (i, 0, 0, 0)),
            pl.BlockSpec((LANES, LANES), lambda i: (0, 0)),
            pl.BlockSpec((rows, rows), lambda i: (0, 0)),
        ],
        out_specs=[pl.BlockSpec((1, rows, LANES), lambda i: (i, 0, 0))] * 2,
        out_shape=[jax.ShapeDtypeStruct((b, rows, LANES), F32)] * 2,
        compiler_params=_cparams(("parallel",)),
        name="route",
    )(aff4, tri, btri)


def _pad_rows(a, rows):
    if a.shape[0] == rows:
        return a
    return jnp.concatenate([a, jnp.zeros((rows - a.shape[0], a.shape[1]), a.dtype)], axis=0)


def _compact_kernel(cin_ref, start_ref, aff_ref, gatec_ref, idxr_ref, *, cap):
    r = _pad_rows(cin_ref[0, 0], LANES)
    st = _pad_rows(start_ref[0, 0], LANES)
    en = st + r[:, LANES - 1:LANES]
    st_row = st.T[0:1, :]
    en_row = en.T[0:1, :]
    sig = lax.broadcasted_iota(jnp.int32, (cap, LANES), 0).astype(F32)
    lane = lax.broadcasted_iota(jnp.int32, (cap, LANES), 1).astype(F32)
    onehot = jnp.where(jnp.logical_and(st_row <= sig, sig < en_row), 1.0, 0.0)
    st_sel = jnp.sum(onehot * st_row, axis=1, keepdims=True)
    c_sel = jnp.sum(onehot * lane, axis=1, keepdims=True)
    oh16 = onehot.astype(BF16)
    r_sel = jnp.dot(oh16, r.astype(BF16), preferred_element_type=F32)
    before = sig[:, 0:1] - st_sel
    pos = jnp.sum(jnp.where(r_sel <= before, 1.0, 0.0), axis=1, keepdims=True)
    idx = c_sel * LANES + pos
    a = _pad_rows(aff_ref[0, 0], LANES)
    a1 = a.astype(BF16)
    rem = a - a1.astype(F32)
    a2 = rem.astype(BF16)
    a3 = (rem - a2.astype(F32)).astype(BF16)
    a_sel = (jnp.dot(oh16, a1, preferred_element_type=F32) + jnp.dot(oh16, a2, preferred_element_type=F32)
             + jnp.dot(oh16, a3, preferred_element_type=F32))
    gatec_ref[0] = jnp.sum(jnp.where(lane == pos, a_sel, 0.0), axis=1, keepdims=True)
    idx_b = jnp.broadcast_to(idx, (cap, LANES))
    for kb in range(cap // LANES):
        blk = idx_b[kb * LANES:(kb + 1) * LANES, :].T
        idxr_ref[0, :, kb * LANES:(kb + 1) * LANES] = blk[0:1, :].astype(jnp.int32)


def _compact(cin, start, afft, cap):
    b, n_exp, t = afft.shape
    nch = t // LANES
    cin4 = cin.reshape(b, n_exp, nch, LANES)
    start4 = start.reshape(b, n_exp, nch, LANES)
    aff4 = afft.reshape(b, n_exp, nch, LANES)
    in_spec = pl.BlockSpec((1, 1, nch, LANES), lambda bi, e: (bi, e, 0, 0))
    kern = functools.partial(_compact_kernel, cap=cap)
    return pl.pallas_call(
        kern,
        grid=(b, n_exp),
        in_specs=[in_spec, in_spec, in_spec],
        out_specs=[
            pl.BlockSpec((1, cap, 1), lambda bi, e: (bi * n_exp + e, 0, 0)),
            pl.BlockSpec((1, 1, cap), lambda bi, e: (bi * n_exp + e, 0, 0)),
        ],
        out_shape=[
            jax.ShapeDtypeStruct((b * n_exp, cap, 1), F32),
            jax.ShapeDtypeStruct((b * n_exp, 1, cap), jnp.int32),
        ],
        compiler_params=_cparams(("parallel", "parallel")),
        name="compact",
    )(cin4, start4, aff4)


MOE_TS = 256
MOE_GK = 8
TOKEN_ROWS = D_MODEL // LANES
GATHER_UNROLL = 8


def _ffn_kernel(idx_ref, h_ref, gate_ref, wg_ref, wu_ref, wd_ref, y_ref, xr_ref):
    tm = gate_ref.shape[1]

    def gather(i, carry):
        for u in range(GATHER_UNROLL):
            r = i * GATHER_UNROLL + u
            src = pl.multiple_of(idx_ref[0, 0, r] * TOKEN_ROWS, TOKEN_ROWS)
            dst = pl.multiple_of(r * TOKEN_ROWS, TOKEN_ROWS)
            xr_ref[pl.ds(dst, TOKEN_ROWS), :] = h_ref[pl.ds(src, TOKEN_ROWS), :]
        return carry

    lax.fori_loop(0, tm // GATHER_UNROLL, gather, 0)
    xe = jnp.concatenate([xr_ref[pl.ds(j, tm, stride=TOKEN_ROWS), :] for j in range(TOKEN_ROWS)],
                         axis=1).astype(BF16)
    a = jnp.dot(xe, wg_ref[0], preferred_element_type=F32)
    u = jnp.dot(xe, wu_ref[0], preferred_element_type=F32)
    act = (a * (1.0 / (1.0 + jnp.exp(-a))) * u).astype(BF16)
    y = jnp.dot(act, wd_ref[0], preferred_element_type=F32)
    y_ref[0] = (y * gate_ref[0]).astype(BF16)


def _ffn(h2r, idx_row, gate_col, w_gate, w_up, w_down, b, t, cap):
    n_exp = w_gate.shape[0]
    tm = min(512, cap)
    wspec = pl.BlockSpec((1, D_MODEL, EXPERT_FF), lambda bi, e, ct: (e, 0, 0))
    return pl.pallas_call(
        _ffn_kernel,
        grid=(b, n_exp, cap // tm),
        in_specs=[
            pl.BlockSpec((1, 1, tm), lambda bi, e, ct: (bi * n_exp + e, 0, ct), memory_space=pltpu.SMEM),
            pl.BlockSpec((t * TOKEN_ROWS, LANES), lambda bi, e, ct: (bi, 0), pipeline_mode=pl.Buffered(1)),
            pl.BlockSpec((1, tm, 1), lambda bi, e, ct: (bi * n_exp + e, ct, 0)),
            wspec, wspec,
            pl.BlockSpec((1, EXPERT_FF, D_MODEL), lambda bi, e, ct: (e, 0, 0)),
        ],
        out_specs=pl.BlockSpec((1, tm, D_MODEL), lambda bi, e, ct: (bi * n_exp + e, ct, 0)),
        out_shape=jax.ShapeDtypeStruct((b * n_exp, cap, D_MODEL), BF16),
        scratch_shapes=[pltpu.VMEM((tm * TOKEN_ROWS, LANES), F32)],
        compiler_params=_cparams(("arbitrary", "arbitrary", "arbitrary")),
        name="expert_ffn",
    )(idx_row, h2r, gate_col, w_gate, w_up, w_down)


def _combine_kernel(klo_ref, khi_ref, x1_ref, y_ref, idxr_ref, nfin_ref, o_ref, ystage_ref, istage_ref, acc_ref, *,
                    cap, final_norm):
    bi, ti = pl.program_id(0), pl.program_id(1)
    nts = pl.num_programs(1)
    ts2 = acc_ref.shape[0]
    n_exp = idxr_ref.shape[1]
    gk = MOE_GK * LANES

    @pl.when(jnp.logical_and(bi == 0, ti == 0))
    def _():
        ystage_ref[...] = jnp.zeros_like(ystage_ref)

    istage_ref[...] = jnp.full(istage_ref.shape, -1, jnp.int32)
    n = jnp.int32(0)
    for e in range(n_exp):
        tab = (bi * nts + ti) * n_exp + e

        def stage(k, n, e=e):
            s0 = pl.multiple_of(k * LANES, LANES)
            d0 = pl.multiple_of(n * LANES, LANES)
            ystage_ref[pl.ds(d0, LANES), :] = y_ref[0, pl.ds(e * cap + s0, LANES), :]
            istage_ref[:, pl.ds(d0, LANES)] = idxr_ref[0, e:e + 1, pl.ds(s0, LANES)]
            return n + 1

        n = lax.fori_loop(klo_ref[tab], khi_ref[tab] + 1, stage, n)
    tok = ti * ts2 + lax.broadcasted_iota(jnp.int32, (ts2, gk), 0)
    acc_ref[...] = jnp.zeros_like(acc_ref)

    def scatter(g, carry):
        k0 = pl.multiple_of(g * gk, gk)
        onehot = jnp.where(tok == istage_ref[:, pl.ds(k0, gk)], 1.0, 0.0).astype(BF16)
        acc_ref[...] += jnp.dot(onehot, ystage_ref[pl.ds(k0, gk), :], preferred_element_type=F32)
        return carry

    lax.fori_loop(0, (n + MOE_GK - 1) // MOE_GK, scatter, 0)
    out = x1_ref[...] + acc_ref[...]
    if final_norm:
        out = out * lax.rsqrt(jnp.mean(out * out, axis=-1, keepdims=True) + RMS_EPS) * nfin_ref[...]
    o_ref[...] = out


def _combine(x1, y, idx_row, start, norm_final, b, t, cap, final_norm):
    n_exp = N_EXPERTS
    ts2 = min(MOE_TS, t)
    nts = t // ts2
    nch = t // LANES
    st = start[:, :, 0].reshape(b, n_exp, nch)[:, :, ::ts2 // LANES].astype(jnp.int32)
    en = jnp.concatenate([st[:, :, 1:], jnp.full((b, n_exp, 1), cap, jnp.int32)], axis=2)
    klo = st // LANES
    khi = jnp.where(en > st, (en - 1) // LANES, klo - 1)
    klo_tab = jnp.transpose(klo, (0, 2, 1)).reshape(-1)
    khi_tab = jnp.transpose(khi, (0, 2, 1)).reshape(-1)
    y3 = y.reshape(b, n_exp * cap, D_MODEL)
    idxr3 = idx_row.reshape(b, n_exp, cap)
    max_chunks = n_exp * min(ts2 // LANES + 1, cap // LANES)
    stage_chunks = -(-max_chunks // MOE_GK) * MOE_GK
    kern = functools.partial(_combine_kernel, cap=cap, final_norm=final_norm)
    grid_spec = pltpu.PrefetchScalarGridSpec(
        num_scalar_prefetch=2,
        grid=(b, nts),
        in_specs=[
            pl.BlockSpec((ts2, D_MODEL), lambda bi, ti, lo, hi: (bi * nts + ti, 0)),
            pl.BlockSpec((1, n_exp * cap, D_MODEL), lambda bi, ti, lo, hi: (bi, 0, 0), pipeline_mode=pl.Buffered(1)),
            pl.BlockSpec((1, n_exp, cap), lambda bi, ti, lo, hi: (bi, 0, 0)),
            pl.BlockSpec((1, D_MODEL), lambda bi, ti, lo, hi: (0, 0)),
        ],
        out_specs=pl.BlockSpec((ts2, D_MODEL), lambda bi, ti, lo, hi: (bi * nts + ti, 0)),
        scratch_shapes=[pltpu.VMEM((stage_chunks * LANES, D_MODEL), BF16),
                        pltpu.VMEM((1, stage_chunks * LANES), jnp.int32),
                        pltpu.VMEM((ts2, D_MODEL), F32)],
    )
    return pl.pallas_call(
        kern,
        grid_spec=grid_spec,
        out_shape=jax.ShapeDtypeStruct((b * t, D_MODEL), F32),
        compiler_params=_cparams(("arbitrary", "arbitrary")),
        name="moe_combine",
    )(klo_tab, khi_tab, x1, y3, idxr3, norm_final.reshape(1, D_MODEL))


GDN_CHUNK = 64


def kernel(x, norm_mix, w_in, conv_w, a_log, dt_bias, gdn_norm, w_branch_a, w_branch_b, w_out, norm_ffn, w_router, w_expert_gate, w_expert_up, w_expert_down, norm_final):
    b, t, d = x.shape
    depth = w_in.shape[0]
    cap = EC_CAPACITY_FACTOR * t // N_EXPERTS
    rope_cos, rope_sin = _rope_tables(t)
    xc = x.reshape(b * t, d)
    for layer in range(depth):
        w_all, w_bd, w_att = _proj_weights(w_in[layer])
        nw = norm_mix[layer].reshape(1, d)
        proj, bd, *att_qkv = _att_proj(xc, nw, w_all, w_bd, w_att, rope_cos, rope_sin, b, t)
        proj3 = proj.reshape(b, t, PROJ_W)
        qkv, aux, gcr = _gdn_prep(proj3, bd.reshape(b, t, LANES), conv_w[layer], a_log[layer], dt_bias[layer],
                                  GDN_CHUNK)
        o_f, o_b = _gdn_scan(qkv, aux, gcr, GDN_CHUNK)
        y_b = _attention(att_qkv, b, t)
        x1, h2r, afft = _merge(o_f.reshape(b * t, GDN_W), o_b.reshape(b * t, GDN_W), proj, y_b, xc,
                              gdn_norm[layer], w_branch_a[layer].astype(BF16), w_branch_b[layer].astype(BF16),
                              w_out[layer].astype(BF16), norm_ffn[layer], w_router[layer], t)
        cin, start = _route(afft, cap)
        gate_col, idx_row = _compact(cin, start, afft, cap)
        y = _ffn(h2r, idx_row, gate_col, w_expert_gate[layer].astype(BF16),
                 w_expert_up[layer].astype(BF16), w_expert_down[layer].astype(BF16), b, t, cap)
        xc = _combine(x1, y, idx_row, start, norm_final, b, t, cap, final_norm=(layer == depth - 1))
    return xc.reshape(b, t, d)
```

```python
import functools

import numpy as np
import jax
import jax.numpy as jnp
from jax import lax
from jax.experimental import pallas as pl
from jax.experimental.pallas import tpu as pltpu

F32 = jnp.float32
BF16 = jnp.bfloat16

D_MODEL = 1024
GDN_HEADS = 4
GDN_DK = 128
GDN_DV = 128
CONV_WIDTH = 5
ATT_GROUPS = ((128, 1), (512, 4), (2048, 16))
ATT_HPG = 4
ATT_HEAD_DIM = 64
ATT_SPAN = 64
ROPE_THETA = 500000.0
ROPE_DIMS = ATT_HEAD_DIM // 4
N_EXPERTS = 16
EXPERT_FF = 1024
EC_CAPACITY_FACTOR = 2
RMS_EPS = 1e-6
NEG_BIG = -1e30

GDN_W = GDN_HEADS * GDN_DK
GDN_CONV_CH = 3 * GDN_W
ATT_GW = ATT_HPG * ATT_HEAD_DIM
ATT_W = len(ATT_GROUPS) * ATT_GW

C_QKVA = 0
C_Z = C_QKVA + GDN_CONV_CH
C_GATE = C_Z + GDN_W
PROJ_W = C_GATE + 2 * D_MODEL
PROJ_TN = 512
BD_W = 4 * GDN_HEADS

LANES = 128
VMEM_LIMIT = 56 * 1024 * 1024


def _cparams(sem):
    return pltpu.CompilerParams(dimension_semantics=sem, vmem_limit_bytes=VMEM_LIMIT)


def _att_proj_kernel(x_ref, nw_ref, wall_ref, wbd_ref, w_ref, rc_ref, rs_ref,
                     o_ref, bd_ref, o0_ref, o1_ref, o2_ref, scr_ref):
    x = x_ref[...]
    tm = x.shape[0]
    ms = jnp.mean(x * x, axis=-1, keepdims=True)
    h = (x * lax.rsqrt(ms + RMS_EPS) * nw_ref[...]).astype(BF16)
    for j in range(PROJ_W // PROJ_TN):
        cols = slice(j * PROJ_TN, (j + 1) * PROJ_TN)
        o_ref[:, cols] = jnp.dot(h, wall_ref[:, cols], preferred_element_type=F32).astype(BF16)
    bd_ref[...] = jnp.dot(h, wbd_ref[...], preferred_element_type=F32)
    reps = ATT_GW // LANES
    cos = jnp.concatenate([rc_ref[...]] * reps, axis=1)
    sin = jnp.concatenate([rs_ref[...]] * reps, axis=1)
    lane = lax.broadcasted_iota(jnp.int32, (tm, ATT_GW), 1) % ATT_HEAD_DIM
    half = ROPE_DIMS // 2
    outs = (o0_ref, o1_ref, o2_ref)
    for g, (_, r) in enumerate(ATT_GROUPS):
        for which in range(3):
            c0 = which * ATT_W + g * ATT_GW
            acc = jnp.dot(h, w_ref[:, c0:c0 + ATT_GW], preferred_element_type=F32)
            if which < 2:
                partner = jnp.where(lane < half, pltpu.roll(acc, ATT_GW - half, 1), pltpu.roll(acc, half, 1))
                acc = acc * cos + partner * sin
            cols = slice(which * ATT_GW, (which + 1) * ATT_GW)
            if r == 1:
                outs[g][0, 0, :, cols] = acc.astype(BF16)
            else:
                for hf in range(reps):
                    scr_ref[hf] = acc[:, hf * LANES:(hf + 1) * LANES]
                for s in range(r):
                    sub = [scr_ref[hf, pl.ds(s, tm // r, stride=r), :] for hf in range(reps)]
                    outs[g][0, s, :, cols] = jnp.concatenate(sub, axis=1).astype(BF16)


def _att_proj(x2d, norm_w, w_all, w_bd, w_att, rope_cos, rope_sin, b, seq):
    n = x2d.shape[0]
    tm = min(512, seq)
    tpb = seq // tm
    out_specs = [pl.BlockSpec((tm, PROJ_W), lambda i: (i, 0)), pl.BlockSpec((tm, LANES), lambda i: (i, 0))]
    out_shape = [jax.ShapeDtypeStruct((n, PROJ_W), BF16), jax.ShapeDtypeStruct((n, LANES), F32)]
    for _, r in ATT_GROUPS:
        out_specs.append(pl.BlockSpec((1, r, tm // r, 3 * ATT_GW), lambda i: (i // tpb, 0, i % tpb, 0)))
        out_shape.append(jax.ShapeDtypeStruct((b, r, seq // r, 3 * ATT_GW), BF16))
    return pl.pallas_call(
        _att_proj_kernel,
        grid=(n // tm,),
        in_specs=[
            pl.BlockSpec((tm, D_MODEL), lambda i: (i, 0)),
            pl.BlockSpec((1, D_MODEL), lambda i: (0, 0)),
            pl.BlockSpec((D_MODEL, PROJ_W), lambda i: (0, 0), pipeline_mode=pl.Buffered(1)),
            pl.BlockSpec((D_MODEL, LANES), lambda i: (0, 0)),
            pl.BlockSpec((D_MODEL, 3 * ATT_W), lambda i: (0, 0), pipeline_mode=pl.Buffered(1)),
            pl.BlockSpec((tm, LANES), lambda i: (i % tpb, 0)),
            pl.BlockSpec((tm, LANES), lambda i: (i % tpb, 0)),
        ],
        out_specs=out_specs,
        out_shape=out_shape,
        scratch_shapes=[pltpu.VMEM((ATT_GW // LANES, tm, LANES), F32)],
        compiler_params=_cparams(("parallel",)),
        name="in_proj",
    )(x2d, norm_w, w_all, w_bd, w_att, rope_cos, rope_sin)


def _rope_tables(seq):
    half = ROPE_DIMS // 2
    inv_freq = jnp.power(ROPE_THETA, -jnp.arange(half, dtype=F32) * 2.0 / ROPE_DIMS)
    ang = jnp.arange(seq, dtype=F32)[:, None] * inv_freq[None, :]
    cos, sin = jnp.cos(ang), jnp.sin(ang)
    rest = ATT_HEAD_DIM - ROPE_DIMS
    cos_h = jnp.concatenate([cos, cos, jnp.ones((seq, rest), F32)], axis=1)
    sin_h = jnp.concatenate([-sin, sin, jnp.zeros((seq, rest), F32)], axis=1)
    reps = LANES // ATT_HEAD_DIM
    return jnp.tile(cos_h, (1, reps)), jnp.tile(sin_h, (1, reps))


def _proj_weights(w_in):
    off_z = GDN_CONV_CH
    off_beta = off_z + GDN_W
    off_att = off_beta + 4 * GDN_HEADS
    off_gate = off_att + 3 * ATT_W
    qkva = w_in[:, :off_z]
    z = w_in[:, off_z:off_beta]
    bd = w_in[:, off_beta:off_att]
    att = w_in[:, off_att:off_gate]
    gates = w_in[:, off_gate:]
    att = jnp.concatenate([att[:, :ATT_W] * (ATT_HEAD_DIM ** -0.5), att[:, ATT_W:]], axis=1)
    bd = jnp.concatenate([bd, jnp.zeros((w_in.shape[0], LANES - BD_W), w_in.dtype)], axis=1)
    return jnp.concatenate([qkva, z, gates], axis=1).astype(BF16), bd.astype(BF16), att.astype(BF16)


AUX_BETA = 0
AUX_GC = 8
AUX_EG = 16
AUX_EK = 24
AUX_GE = 32
HALO = 16


def _gdn_prep_kernel(xp_ref, xc_ref, xn_ref, bd_ref, cw_ref, alog_ref, dtb_ref,
                     qkv_ref, aux_ref, gcr_ref, xf_ref, *, chunk):
    i = pl.program_id(1)
    nblk = pl.num_programs(1)
    tp = xc_ref.shape[1]
    prev = jnp.where(i > 0, xp_ref[0].astype(F32), 0.0)
    nxt = jnp.where(i < nblk - 1, xn_ref[0].astype(F32), 0.0)
    xf_ref[0:HALO, :] = prev
    xf_ref[HALO:HALO + tp, :] = xc_ref[0].astype(F32)
    xf_ref[HALO + tp:, :] = nxt
    pad = (CONV_WIDTH - 1) // 2
    for hb in range(3 * GDN_HEADS):
        sl = slice(hb * GDN_DK, (hb + 1) * GDN_DK)
        conv = None
        for j in range(CONV_WIDTH):
            term = xf_ref[pl.ds(HALO - pad + j, tp), sl] * cw_ref[j:j + 1, sl]
            conv = term if conv is None else conv + term
        uh = conv * (1.0 / (1.0 + jnp.exp(-conv)))
        if hb < 2 * GDN_HEADS:
            scale = lax.rsqrt(jnp.sum(uh * uh, axis=-1, keepdims=True) + 1e-6)
            if hb < GDN_HEADS:
                scale = scale * (GDN_DK ** -0.5)
            uh = uh * scale
        qkv_ref[0, :, sl] = uh.astype(BF16)

    nh2 = 2 * GDN_HEADS
    bd = bd_ref[0]
    lane = lax.broadcasted_iota(jnp.int32, bd.shape, 1)
    row = lax.broadcasted_iota(jnp.int32, bd.shape, 0) % chunk
    beta = 1.0 / (1.0 + jnp.exp(-bd))
    sp_in = bd + dtb_ref[...]
    softplus = jnp.maximum(sp_in, 0.0) + jnp.log(1.0 + jnp.exp(-jnp.abs(sp_in)))
    g = -jnp.exp(alog_ref[...]) * softplus
    g = jnp.where(jnp.logical_and(lane >= nh2, lane < 2 * nh2), g, 0.0)
    pre = g
    suf = g
    s = 1
    while s < chunk:
        pre = pre + jnp.where(row >= s, pltpu.roll(pre, s, 0), 0.0)
        suf = suf + jnp.where(row < chunk - s, pltpu.roll(suf, tp - s, 0), 0.0)
        s *= 2
    total = pre + suf - g
    is_fwd = lane < nh2 + GDN_HEADS
    gc = jnp.where(is_fwd, pre, suf)
    eg = jnp.exp(gc)
    ek = jnp.exp(total - gc)
    ge = jnp.exp(total)
    aux = jnp.where(lane < nh2, beta, 0.0)
    aux = aux + gc
    aux = aux + jnp.where(jnp.logical_and(lane >= AUX_EG, lane < AUX_EG + nh2), pltpu.roll(eg, AUX_EG - AUX_GC, 1), 0.0)
    aux = aux + jnp.where(jnp.logical_and(lane >= AUX_EK, lane < AUX_EK + nh2), pltpu.roll(ek, AUX_EK - AUX_GC, 1), 0.0)
    aux = aux + jnp.where(jnp.logical_and(lane >= AUX_GE, lane < AUX_GE + nh2), pltpu.roll(ge, AUX_GE - AUX_GC, 1), 0.0)
    aux_ref[0] = aux
    gct = gc.T
    for c in range(tp // chunk):
        g = gct[AUX_GC:AUX_GC + nh2, c * chunk:(c + 1) * chunk]
        gcr_ref[0, c] = jnp.concatenate([g, g], axis=1)


def _gdn_prep(proj3, bd3, conv_w, a_log, dt_bias, chunk):
    b, t, _ = proj3.shape
    tp = min(256, t)
    nblk = t // tp
    nh2 = 2 * GDN_HEADS
    alog_row = jnp.zeros((1, LANES), F32).at[0, nh2:2 * nh2].set(a_log.reshape(-1))
    dtb_row = jnp.zeros((1, LANES), F32).at[0, nh2:2 * nh2].set(dt_bias.reshape(-1))
    hb = tp // HALO
    kern = functools.partial(_gdn_prep_kernel, chunk=chunk)
    return pl.pallas_call(
        kern,
        grid=(b, nblk),
        in_specs=[
            pl.BlockSpec((1, HALO, GDN_CONV_CH), lambda bi, i: (bi, jnp.maximum(i * hb - 1, 0), 0)),
            pl.BlockSpec((1, tp, GDN_CONV_CH), lambda bi, i: (bi, i, 0)),
            pl.BlockSpec((1, HALO, GDN_CONV_CH), lambda bi, i: (bi, jnp.minimum((i + 1) * hb, t // HALO - 1), 0)),
            pl.BlockSpec((1, tp, LANES), lambda bi, i: (bi, i, 0)),
            pl.BlockSpec((CONV_WIDTH, GDN_CONV_CH), lambda bi, i: (0, 0)),
            pl.BlockSpec((1, LANES), lambda bi, i: (0, 0)),
            pl.BlockSpec((1, LANES), lambda bi, i: (0, 0)),
        ],
        out_specs=[
            pl.BlockSpec((1, tp, GDN_CONV_CH), lambda bi, i: (bi, i, 0)),
            pl.BlockSpec((1, tp, LANES), lambda bi, i: (bi, i, 0)),
            pl.BlockSpec((1, tp // chunk, nh2, 2 * chunk), lambda bi, i: (bi, i, 0, 0)),
        ],
        out_shape=[
            jax.ShapeDtypeStruct((b, t, GDN_CONV_CH), BF16),
            jax.ShapeDtypeStruct((b, t, LANES), F32),
            jax.ShapeDtypeStruct((b, t // chunk, nh2, 2 * chunk), F32),
        ],
        scratch_shapes=[pltpu.VMEM((tp + 2 * HALO, GDN_CONV_CH), F32)],
        compiler_params=_cparams(("parallel", "parallel")),
        name="gdn_prep",
    )(proj3, proj3, proj3, bd3, conv_w, alog_row, dtb_row)


GDN_CHUNKS_PER_ITER = 4
_DN_T = (((1,), (1,)), ((), ()))
_DN_L = (((0,), (0,)), ((), ()))


def _block_diag(pair, is_b):
    zero = jnp.zeros_like(pair)
    return jnp.concatenate([jnp.where(is_b, zero, pair), jnp.where(is_b, pair, zero)], axis=0)


def _gdn_intra_step(pairs, chunk):
    assert 2 * chunk == LANES
    ri = lax.broadcasted_iota(jnp.int32, (chunk, LANES), 0)
    lane = lax.broadcasted_iota(jnp.int32, (chunk, LANES), 1)
    is_b = lane >= chunk
    ci = jnp.where(is_b, lane - chunk, lane)
    below = jnp.where(is_b, ci - ri, ri - ci)
    incl = below >= 0
    strict = below > 0
    eye = jnp.where(ri == ci, 1.0, 0.0)
    for pr in pairs:
        for un in (pr["f"], pr["b"]):
            col, aux = un["col"], un["aux"]
            beta = aux[:, AUX_BETA + col:AUX_BETA + col + 1]
            eg = aux[:, AUX_EG + col:AUX_EG + col + 1]
            ek = aux[:, AUX_EK + col:AUX_EK + col + 1]
            un["gc"] = aux[:, AUX_GC + col:AUX_GC + col + 1]
            un["ge"] = aux[0:1, AUX_GE + col:AUX_GE + col + 1]
            un["kb"] = un["k"] * beta
            un["rhs"] = jnp.concatenate([un["v"] * beta, un["kb"] * eg], axis=1).astype(BF16)
            un["qd"] = un["q"] * eg
            un["kd"] = (un["k"] * ek).astype(BF16)
        f, b = pr["f"], pr["b"]
        gc = jnp.where(is_b, b["gc"], f["gc"])
        gcr = jnp.where(is_b[0:1], b["gcr"], f["gcr"])
        pr["decay"] = jnp.where(incl, jnp.exp(jnp.minimum(gc - gcr, 0.0)), 0.0)
        zk = jnp.zeros_like(f["k"])
        pr["kbq"] = jnp.concatenate([jnp.concatenate([f["kb"], b["kb"]], axis=1),
                                     jnp.concatenate([f["q"], b["q"]], axis=1)], axis=0).astype(BF16)
        pr["kt"] = jnp.concatenate([jnp.concatenate([f["k"], zk], axis=1),
                                    jnp.concatenate([zk, b["k"]], axis=1)], axis=0).astype(BF16)
        pr["rhs2"] = jnp.concatenate([f["rhs"], b["rhs"]], axis=0)
    for pr in pairs:
        pr["kkqk"] = lax.dot_general(pr["kbq"], pr["kt"], _DN_T, preferred_element_type=F32)
    for pr in pairs:
        lmat = jnp.where(strict, pr["kkqk"][:chunk] * pr["decay"], 0.0)
        pr["amat"] = _block_diag(pr["kkqk"][chunk:] * pr["decay"], is_b).astype(BF16)
        pr["pw"] = lmat
        pr["tinv"] = eye - lmat
    n_sq = max(chunk.bit_length() - 2, 0)
    for i in range(n_sq + 1):
        for pr in pairs:
            pw_bd = _block_diag(pr["pw"], is_b).astype(BF16)
            if i == 0:
                lhs = pr["pw"]
            elif i < n_sq:
                lhs = jnp.concatenate([pr["pw"], pr["tinv"]], axis=0)
            else:
                lhs = pr["tinv"]
            pr["prod"] = jnp.dot(lhs.astype(BF16), pw_bd, preferred_element_type=F32)
        for pr in pairs:
            if i == 0:
                pr["pw"] = pr["prod"]
            elif i < n_sq:
                pr["tinv"] = pr["tinv"] + pr["prod"][chunk:]
                pr["pw"] = pr["prod"][:chunk]
            else:
                pr["tinv"] = pr["tinv"] + pr["prod"]
    for pr in pairs:
        uw2 = jnp.dot(_block_diag(pr["tinv"], is_b).astype(BF16), pr["rhs2"], preferred_element_type=F32)
        pr["f"]["uw"], pr["b"]["uw"] = uw2[:chunk], uw2[chunk:]


def _gdn_state_step(pairs, chunk):
    units = [un for pr in pairs for un in (pr["f"], pr["b"])]
    for un in units:
        lhs = jnp.concatenate([un["uw"][:, GDN_DV:], un["qd"]], axis=0).astype(BF16)
        un["ws"] = jnp.dot(lhs, un["s"].astype(BF16), preferred_element_type=F32)
    for un in units:
        un["v_new"] = (un["uw"][:, :GDN_DV] - un["ws"][:chunk]).astype(BF16)
    for pr in pairs:
        f, b = pr["f"], pr["b"]
        av = jnp.dot(pr["amat"], jnp.concatenate([f["v_new"], b["v_new"]], axis=0), preferred_element_type=F32)
        f["o"] = f["ws"][chunk:] + av[:chunk]
        b["o"] = b["ws"][chunk:] + av[chunk:]
    for un in units:
        un["s_new"] = un["s"] * un["ge"] + lax.dot_general(un["kd"], un["v_new"], _DN_L, preferred_element_type=F32)


def _gdn_scan_kernel(qkvf_ref, qkvb_ref, auxf_ref, auxb_ref, gcrf_ref, gcrb_ref,
                     of_ref, ob_ref, s_ref, *, chunk):
    @pl.when(pl.program_id(1) == 0)
    def _():
        s_ref[...] = jnp.zeros_like(s_ref)

    nc = qkvf_ref.shape[1] // chunk
    dirs = ((qkvf_ref, auxf_ref, gcrf_ref, of_ref), (qkvb_ref, auxb_ref, gcrb_ref, ob_ref))

    per_iter = GDN_CHUNKS_PER_ITER if nc % GDN_CHUNKS_PER_ITER == 0 else 1

    def body(it, carry):
        groups = []
        for sub in range(per_iter):
            c = it * per_iter + sub
            by_dir = []
            for d, (qkv_ref, aux_ref, gcr_ref, o_ref) in enumerate(dirs):
                cc = c if d == 0 else nc - 1 - c
                r0 = pl.multiple_of(cc * chunk, chunk)
                aux = aux_ref[0, pl.ds(r0, chunk), :]
                gcr_all = gcr_ref[0, cc]
                units = []
                for h in range(GDN_HEADS):
                    si = d * GDN_HEADS + h
                    units.append(dict(
                        q=qkv_ref[0, pl.ds(r0, chunk), h * GDN_DK:(h + 1) * GDN_DK],
                        k=qkv_ref[0, pl.ds(r0, chunk), GDN_W + h * GDN_DK:GDN_W + (h + 1) * GDN_DK],
                        v=qkv_ref[0, pl.ds(r0, chunk), 2 * GDN_W + h * GDN_DV:2 * GDN_W + (h + 1) * GDN_DV],
                        aux=aux, gcr=gcr_all[si:si + 1, :], col=si, d=d, h=h, r0=r0, o_ref=o_ref))
                by_dir.append(units)
            groups.append([dict(f=uf, b=ub) for uf, ub in zip(*by_dir)])
        _gdn_intra_step([pr for pairs in groups for pr in pairs], chunk)
        state = [s_ref[si] for si in range(2 * GDN_HEADS)]
        for pairs in groups:
            units = [un for pr in pairs for un in (pr["f"], pr["b"])]
            for un in units:
                un["s"] = state[un["col"]]
            _gdn_state_step(pairs, chunk)
            for un in units:
                state[un["col"]] = un["s_new"]
                un["o_ref"][0, pl.ds(un["r0"], chunk), un["h"] * GDN_DV:(un["h"] + 1) * GDN_DV] = un["o"].astype(BF16)
        for si in range(2 * GDN_HEADS):
            s_ref[si] = state[si]
        return carry

    lax.fori_loop(0, nc // per_iter, body, 0)


def _gdn_scan(qkv, aux, gcr, chunk):
    b, t, _ = qkv.shape
    tb = min(512, t)
    nb = t // tb
    nh2 = 2 * GDN_HEADS
    kern = functools.partial(_gdn_scan_kernel, chunk=chunk)
    fwd = lambda bi, i: (bi, i, 0)
    bwd = lambda bi, i: (bi, nb - 1 - i, 0)
    return pl.pallas_call(
        kern,
        grid=(b, nb),
        in_specs=[
            pl.BlockSpec((1, tb, GDN_CONV_CH), fwd),
            pl.BlockSpec((1, tb, GDN_CONV_CH), bwd),
            pl.BlockSpec((1, tb, LANES), fwd),
            pl.BlockSpec((1, tb, LANES), bwd),
            pl.BlockSpec((1, tb // chunk, nh2, 2 * chunk), lambda bi, i: (bi, i, 0, 0)),
            pl.BlockSpec((1, tb // chunk, nh2, 2 * chunk), lambda bi, i: (bi, nb - 1 - i, 0, 0)),
        ],
        out_specs=[pl.BlockSpec((1, tb, GDN_W), fwd), pl.BlockSpec((1, tb, GDN_W), bwd)],
        out_shape=[jax.ShapeDtypeStruct((b, t, GDN_W), BF16)] * 2,
        scratch_shapes=[pltpu.VMEM((nh2, GDN_DK, GDN_DV), F32)],
        compiler_params=_cparams(("parallel", "arbitrary")),
        name="gdn_scan",
    )(qkv, qkv, aux, aux, gcr, gcr)


ATT_QT = 128


ATT_BLK = 2048


ATT_TILES_PER_ITER = 2


def _attn_tiles(tiles):
    head_of_lane = lax.broadcasted_iota(jnp.int32, (ATT_QT, ATT_GW), 1) // ATT_HEAD_DIM
    in_head = [head_of_lane == h for h in range(ATT_HPG)]
    for tl in tiles:
        q = tl["q"]
        q4 = jnp.concatenate([jnp.where(m, q, jnp.zeros_like(q)) for m in in_head], axis=0)
        tl["s"] = lax.dot_general(q4, tl["kt"], _DN_T, preferred_element_type=F32)
    for tl in tiles:
        nk = tl["s"].shape[1]
        s = jnp.where(tl["valid"][None], tl["s"].reshape(ATT_HPG, ATT_QT, nk), NEG_BIG)
        mx = jnp.max(s, axis=-1, keepdims=True)
        p = jnp.exp(s - mx)
        tl["l"] = jnp.sum(p, axis=-1, keepdims=True)
        tl["mx"] = mx
        tl["p"] = p.reshape(ATT_HPG * ATT_QT, nk).astype(BF16)
    for tl in tiles:
        tl["pv"] = jnp.dot(tl["p"], tl["vt"], preferred_element_type=F32)
    for tl in tiles:
        o = jnp.zeros((ATT_QT, ATT_GW), F32)
        lse = jnp.zeros((ATT_QT, ATT_GW), F32)
        for h in range(ATT_HPG):
            rows = slice(h * ATT_QT, (h + 1) * ATT_QT)
            o = jnp.where(in_head[h], tl["pv"][rows] / tl["l"][h], o)
            lse = jnp.where(in_head[h], tl["mx"][h] + jnp.log(tl["l"][h]), lse)
        tl["o"], tl["lse"] = o, lse


def _attn_kernel(*refs, seq):
    n_g = len(ATT_GROUPS)
    ins = refs[:7 * n_g]
    y_ref = refs[7 * n_g]
    kv_scr = refs[7 * n_g + 1:7 * n_g + 1 + 2 * n_g]
    o_scr, l_scr = refs[7 * n_g + 1 + 2 * n_g:]
    i = pl.program_id(1)
    n_half = ATT_GW // LANES
    nk = ATT_QT + 2 * ATT_SPAN
    qi = lax.broadcasted_iota(jnp.int32, (ATT_QT, nk), 0)
    ki = lax.broadcasted_iota(jnp.int32, (ATT_QT, nk), 1)
    rel = ki - qi
    band = jnp.logical_and(rel >= 0, rel <= 2 * ATT_SPAN)
    for g, (_, r) in enumerate(ATT_GROUPS):
        q_ref, kp_ref, kc_ref, kn_ref, vp_ref, vc_ref, vn_ref = ins[7 * g:7 * g + 7]
        kf_ref, vf_ref = kv_scr[2 * g:2 * g + 2]
        rows = ATT_BLK // r
        m_total = seq // r
        for full_ref, parts in ((kf_ref, (kp_ref, kc_ref, kn_ref)), (vf_ref, (vp_ref, vc_ref, vn_ref))):
            full_ref[:, 0:ATT_SPAN, :] = parts[0][0]
            full_ref[:, ATT_SPAN:ATT_SPAN + rows, :] = parts[1][0]
            full_ref[:, ATT_SPAN + rows:, :] = parts[2][0]
        tiles = rows // ATT_QT

        assert (r * tiles) % ATT_TILES_PER_ITER == 0

        def tile_group(it, carry, r=r, g=g, rows=rows, tiles=tiles, m_total=m_total,
                       q_ref=q_ref, kf_ref=kf_ref, vf_ref=vf_ref):
            group = []
            for u in range(ATT_TILES_PER_ITER):
                idx = it * ATT_TILES_PER_ITER + u
                s = idx // tiles
                r0 = pl.multiple_of((idx % tiles) * ATT_QT, ATT_QT)
                mk = i * rows + r0 - ATT_SPAN + ki
                group.append(dict(
                    res=s, r0=r0, q=q_ref[0, s, pl.ds(r0, ATT_QT), :],
                    kt=kf_ref[s, pl.ds(r0, nk), :], vt=vf_ref[s, pl.ds(r0, nk), :],
                    valid=jnp.logical_and(band, jnp.logical_and(mk >= 0, mk < m_total))))
            _attn_tiles(group)
            for tl in group:
                for hf in range(n_half):
                    lanes = slice(hf * LANES, (hf + 1) * LANES)
                    if r == 1:
                        dst = pl.ds(tl["r0"], ATT_QT)
                    else:
                        dst = pl.ds(tl["r0"] * r + tl["res"], ATT_QT, stride=r)
                    o_scr[g * n_half + hf, dst, :] = tl["o"][:, lanes]
                    l_scr[g * n_half + hf, dst, :] = tl["lse"][:, lanes]
            return carry

        lax.fori_loop(0, r * tiles // ATT_TILES_PER_ITER, tile_group, 0)

    cb = 256

    def comb(c, carry):
        r0 = pl.multiple_of(c * cb, cb)
        halves = []
        for hf in range(n_half):
            ls = [l_scr[g * n_half + hf, pl.ds(r0, cb), :] for g in range(n_g)]
            lm = functools.reduce(jnp.maximum, ls)
            es = [jnp.exp(lv - lm) for lv in ls]
            num = sum(es[g] * o_scr[g * n_half + hf, pl.ds(r0, cb), :] for g in range(n_g))
            halves.append(num / sum(es))
        y_ref[pl.ds(r0, cb), :] = jnp.concatenate(halves, axis=1).astype(BF16)
        return carry

    lax.fori_loop(0, ATT_BLK // cb, comb, 0)


def _attention(att_qkv, b, seq):
    assert seq % ATT_BLK == 0
    nblk = seq // ATT_BLK
    in_specs, scratch, args = [], [], []
    for (_, r), arr in zip(ATT_GROUPS, att_qkv):
        rows = ATT_BLK // r
        assert rows % ATT_QT == 0
        hb = rows // ATT_SPAN
        last = seq // r // ATT_SPAN - 1

        def center(c, r=r, rows=rows):
            return pl.BlockSpec((1, r, rows, ATT_GW), lambda bi, i: (bi, 0, i, c))

        def prev(c, r=r, hb=hb):
            return pl.BlockSpec((1, r, ATT_SPAN, ATT_GW), lambda bi, i: (bi, 0, jnp.maximum(i * hb - 1, 0), c))

        def nxt(c, r=r, hb=hb, last=last):
            return pl.BlockSpec((1, r, ATT_SPAN, ATT_GW), lambda bi, i: (bi, 0, jnp.minimum((i + 1) * hb, last), c))

        in_specs += [center(0), prev(1), center(1), nxt(1), prev(2), center(2), nxt(2)]
        args += [arr] * 7
        scratch += [pltpu.VMEM((r, rows + 2 * ATT_SPAN, ATT_GW), BF16)] * 2
    n_g = len(ATT_GROUPS)
    scratch += [pltpu.VMEM((n_g * (ATT_GW // LANES), ATT_BLK, LANES), F32)] * 2
    kern = functools.partial(_attn_kernel, seq=seq)
    return pl.pallas_call(
        kern,
        grid=(b, nblk),
        in_specs=in_specs,
        out_specs=pl.BlockSpec((ATT_BLK, ATT_GW), lambda bi, i: (bi * nblk + i, 0)),
        out_shape=jax.ShapeDtypeStruct((b * seq, ATT_GW), BF16),
        scratch_shapes=scratch,
        compiler_params=_cparams(("parallel", "parallel")),
        name="attention",
    )(*args)


def _merge_kernel(of_ref, ob_ref, z_ref, ga_ref, gb_ref, yb_ref,
                  x_ref, gn_ref, wa_ref, wb_ref, wo_ref, nf_ref, wr_ref,
                  x1_ref, h2r_ref, afft_ref):
    o = of_ref[...].astype(F32) + ob_ref[...].astype(F32)
    z = z_ref[...].astype(F32)
    ya_parts = []
    for h in range(GDN_HEADS):
        sl = slice(h * GDN_DV, (h + 1) * GDN_DV)
        oh = o[:, sl]
        oh = oh * lax.rsqrt(jnp.mean(oh * oh, axis=-1, keepdims=True) + RMS_EPS) * gn_ref[...]
        zh = z[:, sl]
        ya_parts.append(oh * (zh * (1.0 / (1.0 + jnp.exp(-zh)))))
    y_a = jnp.concatenate(ya_parts, axis=1).astype(BF16)
    y_b = yb_ref[...]
    ga = 1.0 / (1.0 + jnp.exp(-ga_ref[...].astype(F32)))
    gb = 1.0 / (1.0 + jnp.exp(-gb_ref[...].astype(F32)))
    mixed = (ga * jnp.dot(y_a, wa_ref[...], preferred_element_type=F32)
             + gb * jnp.dot(y_b, wb_ref[...], preferred_element_type=F32))
    x1 = x_ref[...] + jnp.dot(mixed.astype(BF16), wo_ref[...], preferred_element_type=F32)
    x1_ref[...] = x1
    h2f = x1 * lax.rsqrt(jnp.mean(x1 * x1, axis=-1, keepdims=True) + RMS_EPS) * nf_ref[...]
    h2 = h2f.astype(BF16)
    tm = h2f.shape[0]
    for j in range(D_MODEL // LANES):
        h2r_ref[pl.ds(j, tm, stride=D_MODEL // LANES), :] = h2f[:, j * LANES:(j + 1) * LANES]
    logits = jnp.dot(h2, wr_ref[...], preferred_element_type=F32)
    lane = lax.broadcasted_iota(jnp.int32, logits.shape, 1)
    logits = jnp.where(lane < N_EXPERTS, logits, NEG_BIG)
    lmax = jnp.max(logits, axis=-1, keepdims=True)
    ex = jnp.exp(logits - lmax)
    aff = ex / jnp.sum(ex, axis=-1, keepdims=True)
    afft_ref[0] = aff.T[:N_EXPERTS, :]


def _merge(of, ob, proj, y_b, x2d, gdn_norm, wa, wb, wo, norm_ffn, w_router, seq):
    n = x2d.shape[0]
    tm = min(512, seq)
    tpb = seq // tm
    wr = jnp.zeros((D_MODEL, LANES), BF16).at[:, :N_EXPERTS].set(w_router.astype(BF16))
    row = lambda w: pl.BlockSpec((tm, w), lambda i: (i, 0))
    full = lambda a: pl.BlockSpec(a.shape, lambda i: (0,) * a.ndim)
    gn = gdn_norm.reshape(1, GDN_DV)
    nf = norm_ffn.reshape(1, D_MODEL)
    in_specs = [
        row(GDN_W), row(GDN_W),
        pl.BlockSpec((tm, GDN_W), lambda i: (i, C_Z // GDN_W)),
        pl.BlockSpec((tm, D_MODEL), lambda i: (i, C_GATE // D_MODEL)),
        pl.BlockSpec((tm, D_MODEL), lambda i: (i, C_GATE // D_MODEL + 1)),
        row(ATT_GW),
        row(D_MODEL), full(gn), full(wa), full(wb), full(wo), full(nf), full(wr),
    ]
    return pl.pallas_call(
        _merge_kernel,
        grid=(n // tm,),
        in_specs=in_specs,
        out_specs=[
            row(D_MODEL),
            pl.BlockSpec((tm * (D_MODEL // LANES), LANES), lambda i: (i, 0)),
            pl.BlockSpec((1, N_EXPERTS, tm), lambda i: (i // tpb, 0, i % tpb)),
        ],
        out_shape=[
            jax.ShapeDtypeStruct((n, D_MODEL), F32),
            jax.ShapeDtypeStruct((n * (D_MODEL // LANES), LANES), F32),
            jax.ShapeDtypeStruct((n // seq, N_EXPERTS, seq), F32),
        ],
        compiler_params=_cparams(("parallel",)),
        name="merge",
    )(of, ob, proj, proj, proj, y_b, x2d, gn, wa, wb, wo, nf, wr)


def _prefix_tables(n_exp, nch):
    rows = n_exp * nch
    tri = (np.arange(LANES)[:, None] <= np.arange(LANES)[None, :]).astype(np.float32)
    ri = np.arange(rows)
    same = (ri[:, None] // nch) == (ri[None, :] // nch)
    btri = np.logical_and(same, (ri[None, :] % nch) < (ri[:, None] % nch)).astype(np.float32)
    return jnp.asarray(tri, BF16), jnp.asarray(btri, BF16)


def _route_kernel(aff_ref, tri_ref, btri_ref, cin_ref, start_ref, *, cap):
    x = aff_ref[0]
    n_exp, nch, _ = x.shape
    rows = n_exp * nch
    bits = lax.bitcast_convert_type(x, jnp.int32)

    def body(it, ans):
        cand = jnp.bitwise_or(ans, lax.shift_left(jnp.int32(1), 30 - it))
        cnt = jnp.sum(jnp.where(bits >= cand, 1.0, 0.0), axis=(1, 2), keepdims=True)
        return jnp.where(cnt >= cap, cand, ans)

    thr = lax.fori_loop(0, 31, body, jnp.zeros((n_exp, 1, 1), jnp.int32))
    gt = jnp.where(bits > thr, 1.0, 0.0)
    eq = jnp.where(bits == thr, 1.0, 0.0)
    need = cap - jnp.sum(gt, axis=(1, 2), keepdims=True)
    need_rows = jnp.broadcast_to(need, x.shape).reshape(rows, LANES)
    gt2 = gt.reshape(rows, LANES)
    eq2 = eq.reshape(rows, LANES)

    def prefix(mask2):
        cin = jnp.dot(mask2.astype(BF16), tri_ref[...], preferred_element_type=F32)
        tot = jnp.broadcast_to(cin[:, LANES - 1:LANES], cin.shape).astype(BF16)
        off = jnp.dot(btri_ref[...], tot, preferred_element_type=F32)
        return cin, off

    cin_eq, off_eq = prefix(eq2)
    eq_before = off_eq + cin_eq - eq2
    sel = jnp.maximum(gt2, jnp.where(eq_before < need_rows, eq2, 0.0))
    cin, off = prefix(sel)
    cin_ref[0] = cin
    start_ref[0] = off


def _route(afft, cap):
    b, n_exp, t = afft.shape
    nch = t // LANES
    rows = n_exp * nch
    tri, btri = _prefix_tables(n_exp, nch)
    aff4 = afft.reshape(b, n_exp, nch, LANES)
    kern = functools.partial(_route_kernel, cap=cap)
    return pl.pallas_call(
        kern,
        grid=(b,),
        in_specs=[
            pl.BlockSpec((1, n_exp, nch, LANES), lambda i: (i, 0, 0, 0)),
            pl.BlockSpec((LANES, LANES), lambda i: (0, 0)),
            pl.BlockSpec((rows, rows), lambda i: (0, 0)),
        ],
        out_specs=[pl.BlockSpec((1, rows, LANES), lambda i: (i, 0, 0))] * 2,
        out_shape=[jax.ShapeDtypeStruct((b, rows, LANES), F32)] * 2,
        compiler_params=_cparams(("parallel",)),
        name="route",
    )(aff4, tri, btri)


def _pad_rows(a, rows):
    if a.shape[0] == rows:
        return a
    return jnp.concatenate([a, jnp.zeros((rows - a.shape[0], a.shape[1]), a.dtype)], axis=0)


def _compact_kernel(cin_ref, start_ref, aff_ref, gatec_ref, idxr_ref, *, cap):
    r = _pad_rows(cin_ref[0, 0], LANES)
    st = _pad_rows(start_ref[0, 0], LANES)
    en = st + r[:, LANES - 1:LANES]
    st_row = st.T[0:1, :]
    en_row = en.T[0:1, :]
    sig = lax.broadcasted_iota(jnp.int32, (cap, LANES), 0).astype(F32)
    lane = lax.broadcasted_iota(jnp.int32, (cap, LANES), 1).astype(F32)
    onehot = jnp.where(jnp.logical_and(st_row <= sig, sig < en_row), 1.0, 0.0)
    st_sel = jnp.sum(onehot * st_row, axis=1, keepdims=True)
    c_sel = jnp.sum(onehot * lane, axis=1, keepdims=True)
    oh16 = onehot.astype(BF16)
    r_sel = jnp.dot(oh16, r.astype(BF16), preferred_element_type=F32)
    before = sig[:, 0:1] - st_sel
    pos = jnp.sum(jnp.where(r_sel <= before, 1.0, 0.0), axis=1, keepdims=True)
    idx = c_sel * LANES + pos
    a = _pad_rows(aff_ref[0, 0], LANES)
    a1 = a.astype(BF16)
    rem = a - a1.astype(F32)
    a2 = rem.astype(BF16)
    a3 = (rem - a2.astype(F32)).astype(BF16)
    a_sel = (jnp.dot(oh16, a1, preferred_element_type=F32) + jnp.dot(oh16, a2, preferred_element_type=F32)
             + jnp.dot(oh16, a3, preferred_element_type=F32))
    gatec_ref[0] = jnp.sum(jnp.where(lane == pos, a_sel, 0.0), axis=1, keepdims=True)
    idx_b = jnp.broadcast_to(idx, (cap, LANES))
    for kb in range(cap // LANES):
        blk = idx_b[kb * LANES:(kb + 1) * LANES, :].T
        idxr_ref[0, :, kb * LANES:(kb + 1) * LANES] = blk[0:1, :].astype(jnp.int32)


def _compact(cin, start, afft, cap):
    b, n_exp, t = afft.shape
    nch = t // LANES
    cin4 = cin.reshape(b, n_exp, nch, LANES)
    start4 = start.reshape(b, n_exp, nch, LANES)
    aff4 = afft.reshape(b, n_exp, nch, LANES)
    in_spec = pl.BlockSpec((1, 1, nch, LANES), lambda bi, e: (bi, e, 0, 0))
    kern = functools.partial(_compact_kernel, cap=cap)
    return pl.pallas_call(
        kern,
        grid=(b, n_exp),
        in_specs=[in_spec, in_spec, in_spec],
        out_specs=[
            pl.BlockSpec((1, cap, 1), lambda bi, e: (bi * n_exp + e, 0, 0)),
            pl.BlockSpec((1, 1, cap), lambda bi, e: (bi * n_exp + e, 0, 0)),
        ],
        out_shape=[
            jax.ShapeDtypeStruct((b * n_exp, cap, 1), F32),
            jax.ShapeDtypeStruct((b * n_exp, 1, cap), jnp.int32),
        ],
        compiler_params=_cparams(("parallel", "parallel")),
        name="compact",
    )(cin4, start4, aff4)


MOE_TS = 256
MOE_GK = 8
TOKEN_ROWS = D_MODEL // LANES
GATHER_UNROLL = 32


def _ffn_kernel(idx_ref, h_ref, gate_ref, wg_ref, wu_ref, wd_ref, y_ref, xr_ref):
    tm = gate_ref.shape[1]

    def gather(i, carry):
        for u in range(GATHER_UNROLL):
            r = i * GATHER_UNROLL + u
            src = pl.multiple_of(idx_ref[0, 0, r] * TOKEN_ROWS, TOKEN_ROWS)
            dst = pl.multiple_of(r * TOKEN_ROWS, TOKEN_ROWS)
            xr_ref[pl.ds(dst, TOKEN_ROWS), :] = h_ref[pl.ds(src, TOKEN_ROWS), :]
        return carry

    lax.fori_loop(0, tm // GATHER_UNROLL, gather, 0)
    xe = jnp.concatenate([xr_ref[pl.ds(j, tm, stride=TOKEN_ROWS), :] for j in range(TOKEN_ROWS)],
                         axis=1).astype(BF16)
    a = jnp.dot(xe, wg_ref[0], preferred_element_type=F32)
    u = jnp.dot(xe, wu_ref[0], preferred_element_type=F32)
    act = (a * (1.0 / (1.0 + jnp.exp(-a))) * u).astype(BF16)
    y = jnp.dot(act, wd_ref[0], preferred_element_type=F32)
    y_ref[0] = (y * gate_ref[0]).astype(BF16)


def _ffn(h2r, idx_row, gate_col, w_gate, w_up, w_down, b, t, cap):
    n_exp = w_gate.shape[0]
    tm = min(512, cap)
    wspec = pl.BlockSpec((1, D_MODEL, EXPERT_FF), lambda bi, e, ct: (e, 0, 0))
    return pl.pallas_call(
        _ffn_kernel,
        grid=(b, n_exp, cap // tm),
        in_specs=[
            pl.BlockSpec((1, 1, tm), lambda bi, e, ct: (bi * n_exp + e, 0, ct), memory_space=pltpu.SMEM),
            pl.BlockSpec((t * TOKEN_ROWS, LANES), lambda bi, e, ct: (bi, 0), pipeline_mode=pl.Buffered(1)),
            pl.BlockSpec((1, tm, 1), lambda bi, e, ct: (bi * n_exp + e, ct, 0)),
            wspec, wspec,
            pl.BlockSpec((1, EXPERT_FF, D_MODEL), lambda bi, e, ct: (e, 0, 0)),
        ],
        out_specs=pl.BlockSpec((1, tm, D_MODEL), lambda bi, e, ct: (bi * n_exp + e, ct, 0)),
        out_shape=jax.ShapeDtypeStruct((b * n_exp, cap, D_MODEL), BF16),
        scratch_shapes=[pltpu.VMEM((tm * TOKEN_ROWS, LANES), F32)],
        compiler_params=_cparams(("arbitrary", "arbitrary", "arbitrary")),
        name="expert_ffn",
    )(idx_row, h2r, gate_col, w_gate, w_up, w_down)


def _combine_kernel(klo_ref, khi_ref, x1_ref, y_ref, idxr_ref, nfin_ref, o_ref, ystage_ref, istage_ref, acc_ref, *,
                    cap, final_norm):
    bi, ti = pl.program_id(0), pl.program_id(1)
    nts = pl.num_programs(1)
    ts2 = acc_ref.shape[0]
    n_exp = idxr_ref.shape[1]
    gk = MOE_GK * LANES

    @pl.when(jnp.logical_and(bi == 0, ti == 0))
    def _():
        ystage_ref[...] = jnp.zeros_like(ystage_ref)

    istage_ref[...] = jnp.full(istage_ref.shape, -1, jnp.int32)
    n = jnp.int32(0)
    for e in range(n_exp):
        tab = (bi * nts + ti) * n_exp + e

        def stage(k, n, e=e):
            s0 = pl.multiple_of(k * LANES, LANES)
            d0 = pl.multiple_of(n * LANES, LANES)
            ystage_ref[pl.ds(d0, LANES), :] = y_ref[0, pl.ds(e * cap + s0, LANES), :]
            istage_ref[:, pl.ds(d0, LANES)] = idxr_ref[0, e:e + 1, pl.ds(s0, LANES)]
            return n + 1

        n = lax.fori_loop(klo_ref[tab], khi_ref[tab] + 1, stage, n)
    tok = ti * ts2 + lax.broadcasted_iota(jnp.int32, (ts2, gk), 0)
    acc_ref[...] = jnp.zeros_like(acc_ref)

    def scatter(g, carry):
        k0 = pl.multiple_of(g * gk, gk)
        onehot = jnp.where(tok == istage_ref[:, pl.ds(k0, gk)], 1.0, 0.0).astype(BF16)
        acc_ref[...] += jnp.dot(onehot, ystage_ref[pl.ds(k0, gk), :], preferred_element_type=F32)
        return carry

    lax.fori_loop(0, (n + MOE_GK - 1) // MOE_GK, scatter, 0)
    out = x1_ref[...] + acc_ref[...]
    if final_norm:
        out = out * lax.rsqrt(jnp.mean(out * out, axis=-1, keepdims=True) + RMS_EPS) * nfin_ref[...]
    o_ref[...] = out


def _combine(x1, y, idx_row, start, norm_final, b, t, cap, final_norm):
    n_exp = N_EXPERTS
    ts2 = min(MOE_TS, t)
    nts = t // ts2
    nch = t // LANES
    st = start[:, :, 0].reshape(b, n_exp, nch)[:, :, ::ts2 // LANES].astype(jnp.int32)
    en = jnp.concatenate([st[:, :, 1:], jnp.full((b, n_exp, 1), cap, jnp.int32)], axis=2)
    klo = st // LANES
    khi = jnp.where(en > st, (en - 1) // LANES, klo - 1)
    klo_tab = jnp.transpose(klo, (0, 2, 1)).reshape(-1)
    khi_tab = jnp.transpose(khi, (0, 2, 1)).reshape(-1)
    y3 = y.reshape(b, n_exp * cap, D_MODEL)
    idxr3 = idx_row.reshape(b, n_exp, cap)
    max_chunks = n_exp * min(ts2 // LANES + 1, cap // LANES)
    stage_chunks = -(-max_chunks // MOE_GK) * MOE_GK
    kern = functools.partial(_combine_kernel, cap=cap, final_norm=final_norm)
    grid_spec = pltpu.PrefetchScalarGridSpec(
        num_scalar_prefetch=2,
        grid=(b, nts),
        in_specs=[
            pl.BlockSpec((ts2, D_MODEL), lambda bi, ti, lo, hi: (bi * nts + ti, 0)),
            pl.BlockSpec((1, n_exp * cap, D_MODEL), lambda bi, ti, lo, hi: (bi, 0, 0), pipeline_mode=pl.Buffered(1)),
            pl.BlockSpec((1, n_exp, cap), lambda bi, ti, lo, hi: (bi, 0, 0)),
            pl.BlockSpec((1, D_MODEL), lambda bi, ti, lo, hi: (0, 0)),
        ],
        out_specs=pl.BlockSpec((ts2, D_MODEL), lambda bi, ti, lo, hi: (bi * nts + ti, 0)),
        scratch_shapes=[pltpu.VMEM((stage_chunks * LANES, D_MODEL), BF16),
                        pltpu.VMEM((1, stage_chunks * LANES), jnp.int32),
                        pltpu.VMEM((ts2, D_MODEL), F32)],
    )
    return pl.pallas_call(
        kern,
        grid_spec=grid_spec,
        out_shape=jax.ShapeDtypeStruct((b * t, D_MODEL), F32),
        compiler_params=_cparams(("arbitrary", "arbitrary")),
        name="moe_combine",
    )(klo_tab, khi_tab, x1, y3, idxr3, norm_final.reshape(1, D_MODEL))


GDN_CHUNK = 64


def kernel(x, norm_mix, w_in, conv_w, a_log, dt_bias, gdn_norm, w_branch_a, w_branch_b, w_out, norm_ffn, w_router, w_expert_gate, w_expert_up, w_expert_down, norm_final):
    b, t, d = x.shape
    depth = w_in.shape[0]
    cap = EC_CAPACITY_FACTOR * t // N_EXPERTS
    rope_cos, rope_sin = _rope_tables(t)
    xc = x.reshape(b * t, d)
    for layer in range(depth):
        w_all, w_bd, w_att = _proj_weights(w_in[layer])
        nw = norm_mix[layer].reshape(1, d)
        proj, bd, *att_qkv = _att_proj(xc, nw, w_all, w_bd, w_att, rope_cos, rope_sin, b, t)
        proj3 = proj.reshape(b, t, PROJ_W)
        qkv, aux, gcr = _gdn_prep(proj3, bd.reshape(b, t, LANES), conv_w[layer], a_log[layer], dt_bias[layer],
                                  GDN_CHUNK)
        o_f, o_b = _gdn_scan(qkv, aux, gcr, GDN_CHUNK)
        y_b = _attention(att_qkv, b, t)
        x1, h2r, afft = _merge(o_f.reshape(b * t, GDN_W), o_b.reshape(b * t, GDN_W), proj, y_b, xc,
                              gdn_norm[layer], w_branch_a[layer].astype(BF16), w_branch_b[layer].astype(BF16),
                              w_out[layer].astype(BF16), norm_ffn[layer], w_router[layer], t)
        cin, start = _route(afft, cap)
        gate_col, idx_row = _compact(cin, start, afft, cap)
        y = _ffn(h2r, idx_row, gate_col, w_expert_gate[layer].astype(BF16),
                 w_expert_up[layer].astype(BF16), w_expert_down[layer].astype(BF16), b, t, cap)
        xc = _combine(x1, y, idx_row, start, norm_final, b, t, cap, final_norm=(layer == depth - 1))
    return xc.reshape(b, t, d)
```

```python
import functools

import numpy as np
import jax
import jax.numpy as jnp
from jax import lax
from jax.experimental import pallas as pl
from jax.experimental.pallas import tpu as pltpu

F32 = jnp.float32
BF16 = jnp.bfloat16

D_MODEL = 1024
GDN_HEADS = 4
GDN_DK = 128
GDN_DV = 128
CONV_WIDTH = 5
ATT_GROUPS = ((128, 1), (512, 4), (2048, 16))
ATT_HPG = 4
ATT_HEAD_DIM = 64
ATT_SPAN = 64
ROPE_THETA = 500000.0
ROPE_DIMS = ATT_HEAD_DIM // 4
N_EXPERTS = 16
EXPERT_FF = 1024
EC_CAPACITY_FACTOR = 2
RMS_EPS = 1e-6
NEG_BIG = -1e30

GDN_W = GDN_HEADS * GDN_DK
GDN_CONV_CH = 3 * GDN_W
ATT_GW = ATT_HPG * ATT_HEAD_DIM
ATT_W = len(ATT_GROUPS) * ATT_GW

C_QKVA = 0
C_Z = C_QKVA + GDN_CONV_CH
C_GATE = C_Z + GDN_W
PROJ_W = C_GATE + 2 * D_MODEL
PROJ_TN = 512
BD_W = 4 * GDN_HEADS

LANES = 128
VMEM_LIMIT = 56 * 1024 * 1024


def _cparams(sem):
    return pltpu.CompilerParams(dimension_semantics=sem, vmem_limit_bytes=VMEM_LIMIT)


def _att_proj_kernel(x_ref, nw_ref, wall_ref, wbd_ref, w_ref, rc_ref, rs_ref,
                     o_ref, bd_ref, o0_ref, o1_ref, o2_ref, scr_ref):
    x = x_ref[...]
    tm = x.shape[0]
    ms = jnp.mean(x * x, axis=-1, keepdims=True)
    h = (x * lax.rsqrt(ms + RMS_EPS) * nw_ref[...]).astype(BF16)
    for j in range(PROJ_W // PROJ_TN):
        cols = slice(j * PROJ_TN, (j + 1) * PROJ_TN)
        o_ref[:, cols] = jnp.dot(h, wall_ref[:, cols], preferred_element_type=F32).astype(BF16)
    bd_ref[...] = jnp.dot(h, wbd_ref[...], preferred_element_type=F32)
    reps = ATT_GW // LANES
    cos = jnp.concatenate([rc_ref[...]] * reps, axis=1)
    sin = jnp.concatenate([rs_ref[...]] * reps, axis=1)
    lane = lax.broadcasted_iota(jnp.int32, (tm, ATT_GW), 1) % ATT_HEAD_DIM
    half = ROPE_DIMS // 2
    outs = (o0_ref, o1_ref, o2_ref)
    for g, (_, r) in enumerate(ATT_GROUPS):
        for which in range(3):
            c0 = which * ATT_W + g * ATT_GW
            acc = jnp.dot(h, w_ref[:, c0:c0 + ATT_GW], preferred_element_type=F32)
            if which < 2:
                partner = jnp.where(lane < half, pltpu.roll(acc, ATT_GW - half, 1), pltpu.roll(acc, half, 1))
                acc = acc * cos + partner * sin
            cols = slice(which * ATT_GW, (which + 1) * ATT_GW)
            if r == 1:
                outs[g][0, 0, :, cols] = acc.astype(BF16)
            else:
                for hf in range(reps):
                    scr_ref[hf] = acc[:, hf * LANES:(hf + 1) * LANES]
                for s in range(r):
                    sub = [scr_ref[hf, pl.ds(s, tm // r, stride=r), :] for hf in range(reps)]
                    outs[g][0, s, :, cols] = jnp.concatenate(sub, axis=1).astype(BF16)


def _att_proj(x2d, norm_w, w_all, w_bd, w_att, rope_cos, rope_sin, b, seq):
    n = x2d.shape[0]
    tm = min(512, seq)
    tpb = seq // tm
    out_specs = [pl.BlockSpec((tm, PROJ_W), lambda i: (i, 0)), pl.BlockSpec((tm, LANES), lambda i: (i, 0))]
    out_shape = [jax.ShapeDtypeStruct((n, PROJ_W), BF16), jax.ShapeDtypeStruct((n, LANES), F32)]
    for _, r in ATT_GROUPS:
        out_specs.append(pl.BlockSpec((1, r, tm // r, 3 * ATT_GW), lambda i: (i // tpb, 0, i % tpb, 0)))
        out_shape.append(jax.ShapeDtypeStruct((b, r, seq // r, 3 * ATT_GW), BF16))
    return pl.pallas_call(
        _att_proj_kernel,
        grid=(n // tm,),
        in_specs=[
            pl.BlockSpec((tm, D_MODEL), lambda i: (i, 0)),
            pl.BlockSpec((1, D_MODEL), lambda i: (0, 0)),
            pl.BlockSpec((D_MODEL, PROJ_W), lambda i: (0, 0), pipeline_mode=pl.Buffered(1)),
            pl.BlockSpec((D_MODEL, LANES), lambda i: (0, 0)),
            pl.BlockSpec((D_MODEL, 3 * ATT_W), lambda i: (0, 0), pipeline_mode=pl.Buffered(1)),
            pl.BlockSpec((tm, LANES), lambda i: (i % tpb, 0)),
            pl.BlockSpec((tm, LANES), lambda i: (i % tpb, 0)),
        ],
        out_specs=out_specs,
        out_shape=out_shape,
        scratch_shapes=[pltpu.VMEM((ATT_GW // LANES, tm, LANES), F32)],
        compiler_params=_cparams(("parallel",)),
        name="in_proj",
    )(x2d, norm_w, w_all, w_bd, w_att, rope_cos, rope_sin)


def _rope_tables(seq):
    half = ROPE_DIMS // 2
    inv_freq = jnp.power(ROPE_THETA, -jnp.arange(half, dtype=F32) * 2.0 / ROPE_DIMS)
    ang = jnp.arange(seq, dtype=F32)[:, None] * inv_freq[None, :]
    cos, sin = jnp.cos(ang), jnp.sin(ang)
    rest = ATT_HEAD_DIM - ROPE_DIMS
    cos_h = jnp.concatenate([cos, cos, jnp.ones((seq, rest), F32)], axis=1)
    sin_h = jnp.concatenate([-sin, sin, jnp.zeros((seq, rest), F32)], axis=1)
    reps = LANES // ATT_HEAD_DIM
    return jnp.tile(cos_h, (1, reps)), jnp.tile(sin_h, (1, reps))


def _proj_weights(w_in):
    off_z = GDN_CONV_CH
    off_beta = off_z + GDN_W
    off_att = off_beta + 4 * GDN_HEADS
    off_gate = off_att + 3 * ATT_W
    qkva = w_in[:, :off_z]
    z = w_in[:, off_z:off_beta]
    bd = w_in[:, off_beta:off_att]
    att = w_in[:, off_att:off_gate]
    gates = w_in[:, off_gate:]
    att = jnp.concatenate([att[:, :ATT_W] * (ATT_HEAD_DIM ** -0.5), att[:, ATT_W:]], axis=1)
    bd = jnp.concatenate([bd, jnp.zeros((w_in.shape[0], LANES - BD_W), w_in.dtype)], axis=1)
    return jnp.concatenate([qkva, z, gates], axis=1).astype(BF16), bd.astype(BF16), att.astype(BF16)


AUX_BETA = 0
AUX_GC = 8
AUX_EG = 16
AUX_EK = 24
AUX_GE = 32
HALO = 16


def _gdn_prep_kernel(xp_ref, xc_ref, xn_ref, bd_ref, cw_ref, alog_ref, dtb_ref,
                     qkv_ref, aux_ref, gcr_ref, xf_ref, *, chunk):
    i = pl.program_id(1)
    nblk = pl.num_programs(1)
    tp = xc_ref.shape[1]
    prev = jnp.where(i > 0, xp_ref[0].astype(F32), 0.0)
    nxt = jnp.where(i < nblk - 1, xn_ref[0].astype(F32), 0.0)
    xf_ref[0:HALO, :] = prev
    xf_ref[HALO:HALO + tp, :] = xc_ref[0].astype(F32)
    xf_ref[HALO + tp:, :] = nxt
    pad = (CONV_WIDTH - 1) // 2
    for hb in range(3 * GDN_HEADS):
        sl = slice(hb * GDN_DK, (hb + 1) * GDN_DK)
        conv = None
        for j in range(CONV_WIDTH):
            term = xf_ref[pl.ds(HALO - pad + j, tp), sl] * cw_ref[j:j + 1, sl]
            conv = term if conv is None else conv + term
        uh = conv * (1.0 / (1.0 + jnp.exp(-conv)))
        if hb < 2 * GDN_HEADS:
            scale = lax.rsqrt(jnp.sum(uh * uh, axis=-1, keepdims=True) + 1e-6)
            if hb < GDN_HEADS:
                scale = scale * (GDN_DK ** -0.5)
            uh = uh * scale
        qkv_ref[0, :, sl] = uh.astype(BF16)

    nh2 = 2 * GDN_HEADS
    bd = bd_ref[0]
    lane = lax.broadcasted_iota(jnp.int32, bd.shape, 1)
    row = lax.broadcasted_iota(jnp.int32, bd.shape, 0) % chunk
    beta = 1.0 / (1.0 + jnp.exp(-bd))
    sp_in = bd + dtb_ref[...]
    softplus = jnp.maximum(sp_in, 0.0) + jnp.log(1.0 + jnp.exp(-jnp.abs(sp_in)))
    g = -jnp.exp(alog_ref[...]) * softplus
    g = jnp.where(jnp.logical_and(lane >= nh2, lane < 2 * nh2), g, 0.0)
    pre = g
    suf = g
    s = 1
    while s < chunk:
        pre = pre + jnp.where(row >= s, pltpu.roll(pre, s, 0), 0.0)
        suf = suf + jnp.where(row < chunk - s, pltpu.roll(suf, tp - s, 0), 0.0)
        s *= 2
    total = pre + suf - g
    is_fwd = lane < nh2 + GDN_HEADS
    gc = jnp.where(is_fwd, pre, suf)
    eg = jnp.exp(gc)
    ek = jnp.exp(total - gc)
    ge = jnp.exp(total)
    aux = jnp.where(lane < nh2, beta, 0.0)
    aux = aux + gc
    aux = aux + jnp.where(jnp.logical_and(lane >= AUX_EG, lane < AUX_EG + nh2), pltpu.roll(eg, AUX_EG - AUX_GC, 1), 0.0)
    aux = aux + jnp.where(jnp.logical_and(lane >= AUX_EK, lane < AUX_EK + nh2), pltpu.roll(ek, AUX_EK - AUX_GC, 1), 0.0)
    aux = aux + jnp.where(jnp.logical_and(lane >= AUX_GE, lane < AUX_GE + nh2), pltpu.roll(ge, AUX_GE - AUX_GC, 1), 0.0)
    aux_ref[0] = aux
    gct = gc.T
    for c in range(tp // chunk):
        g = gct[AUX_GC:AUX_GC + nh2, c * chunk:(c + 1) * chunk]
        gcr_ref[0, c] = jnp.concatenate([g, g], axis=1)


def _gdn_prep(proj3, bd3, conv_w, a_log, dt_bias, chunk):
    b, t, _ = proj3.shape
    tp = min(256, t)
    nblk = t // tp
    nh2 = 2 * GDN_HEADS
    alog_row = jnp.zeros((1, LANES), F32).at[0, nh2:2 * nh2].set(a_log.reshape(-1))
    dtb_row = jnp.zeros((1, LANES), F32).at[0, nh2:2 * nh2].set(dt_bias.reshape(-1))
    hb = tp // HALO
    kern = functools.partial(_gdn_prep_kernel, chunk=chunk)
    return pl.pallas_call(
        kern,
        grid=(b, nblk),
        in_specs=[
            pl.BlockSpec((1, HALO, GDN_CONV_CH), lambda bi, i: (bi, jnp.maximum(i * hb - 1, 0), 0)),
            pl.BlockSpec((1, tp, GDN_CONV_CH), lambda bi, i: (bi, i, 0)),
            pl.BlockSpec((1, HALO, GDN_CONV_CH), lambda bi, i: (bi, jnp.minimum((i + 1) * hb, t // HALO - 1), 0)),
            pl.BlockSpec((1, tp, LANES), lambda bi, i: (bi, i, 0)),
            pl.BlockSpec((CONV_WIDTH, GDN_CONV_CH), lambda bi, i: (0, 0)),
            pl.BlockSpec((1, LANES), lambda bi, i: (0, 0)),
            pl.BlockSpec((1, LANES), lambda bi, i: (0, 0)),
        ],
        out_specs=[
            pl.BlockSpec((1, tp, GDN_CONV_CH), lambda bi, i: (bi, i, 0)),
            pl.BlockSpec((1, tp, LANES), lambda bi, i: (bi, i, 0)),
            pl.BlockSpec((1, tp // chunk, nh2, 2 * chunk), lambda bi, i: (bi, i, 0, 0)),
        ],
        out_shape=[
            jax.ShapeDtypeStruct((b, t, GDN_CONV_CH), BF16),
            jax.ShapeDtypeStruct((b, t, LANES), F32),
            jax.ShapeDtypeStruct((b, t // chunk, nh2, 2 * chunk), F32),
        ],
        scratch_shapes=[pltpu.VMEM((tp + 2 * HALO, GDN_CONV_CH), F32)],
        compiler_params=_cparams(("parallel", "parallel")),
        name="gdn_prep",
    )(proj3, proj3, proj3, bd3, conv_w, alog_row, dtb_row)


GDN_CHUNKS_PER_ITER = 4
_DN_T = (((1,), (1,)), ((), ()))
_DN_L = (((0,), (0,)), ((), ()))


def _block_diag(pair, is_b):
    zero = jnp.zeros_like(pair)
    return jnp.concatenate([jnp.where(is_b, zero, pair), jnp.where(is_b, pair, zero)], axis=0)


def _gdn_intra_step(pairs, chunk):
    assert 2 * chunk == LANES
    ri = lax.broadcasted_iota(jnp.int32, (chunk, LANES), 0)
    lane = lax.broadcasted_iota(jnp.int32, (chunk, LANES), 1)
    is_b = lane >= chunk
    ci = jnp.where(is_b, lane - chunk, lane)
    below = jnp.where(is_b, ci - ri, ri - ci)
    incl = below >= 0
    strict = below > 0
    eye = jnp.where(ri == ci, 1.0, 0.0)
    for pr in pairs:
        for un in (pr["f"], pr["b"]):
            col, aux = un["col"], un["aux"]
            beta = aux[:, AUX_BETA + col:AUX_BETA + col + 1]
            eg = aux[:, AUX_EG + col:AUX_EG + col + 1]
            ek = aux[:, AUX_EK + col:AUX_EK + col + 1]
            un["gc"] = aux[:, AUX_GC + col:AUX_GC + col + 1]
            un["ge"] = aux[0:1, AUX_GE + col:AUX_GE + col + 1]
            un["kb"] = un["k"] * beta
            un["rhs"] = jnp.concatenate([un["v"] * beta, un["kb"] * eg], axis=1).astype(BF16)
            un["qd"] = un["q"] * eg
            un["kd"] = (un["k"] * ek).astype(BF16)
        f, b = pr["f"], pr["b"]
        gc = jnp.where(is_b, b["gc"], f["gc"])
        gcr = jnp.where(is_b[0:1], b["gcr"], f["gcr"])
        pr["decay"] = jnp.where(incl, jnp.exp(jnp.minimum(gc - gcr, 0.0)), 0.0)
        zk = jnp.zeros_like(f["k"])
        pr["kbq"] = jnp.concatenate([jnp.concatenate([f["kb"], b["kb"]], axis=1),
                                     jnp.concatenate([f["q"], b["q"]], axis=1)], axis=0).astype(BF16)
        pr["kt"] = jnp.concatenate([jnp.concatenate([f["k"], zk], axis=1),
                                    jnp.concatenate([zk, b["k"]], axis=1)], axis=0).astype(BF16)
        pr["rhs2"] = jnp.concatenate([f["rhs"], b["rhs"]], axis=0)
    for pr in pairs:
        pr["kkqk"] = lax.dot_general(pr["kbq"], pr["kt"], _DN_T, preferred_element_type=F32)
    for pr in pairs:
        lmat = jnp.where(strict, pr["kkqk"][:chunk] * pr["decay"], 0.0)
        pr["amat"] = _block_diag(pr["kkqk"][chunk:] * pr["decay"], is_b).astype(BF16)
        pr["pw"] = lmat
        pr["tinv"] = eye - lmat
    n_sq = max(chunk.bit_length() - 2, 0)
    for i in range(n_sq + 1):
        for pr in pairs:
            pw_bd = _block_diag(pr["pw"], is_b).astype(BF16)
            if i == 0:
                lhs = pr["pw"]
            elif i < n_sq:
                lhs = jnp.concatenate([pr["pw"], pr["tinv"]], axis=0)
            else:
                lhs = pr["tinv"]
            pr["prod"] = jnp.dot(lhs.astype(BF16), pw_bd, preferred_element_type=F32)
        for pr in pairs:
            if i == 0:
                pr["pw"] = pr["prod"]
            elif i < n_sq:
                pr["tinv"] = pr["tinv"] + pr["prod"][chunk:]
                pr["pw"] = pr["prod"][:chunk]
            else:
                pr["tinv"] = pr["tinv"] + pr["prod"]
    for pr in pairs:
        uw2 = jnp.dot(_block_diag(pr["tinv"], is_b).astype(BF16), pr["rhs2"], preferred_element_type=F32)
        pr["f"]["uw"], pr["b"]["uw"] = uw2[:chunk], uw2[chunk:]


def _gdn_state_step(pairs, chunk):
    units = [un for pr in pairs for un in (pr["f"], pr["b"])]
    for un in units:
        lhs = jnp.concatenate([un["uw"][:, GDN_DV:], un["qd"]], axis=0).astype(BF16)
        un["ws"] = jnp.dot(lhs, un["s"].astype(BF16), preferred_element_type=F32)
    for un in units:
        un["v_new"] = (un["uw"][:, :GDN_DV] - un["ws"][:chunk]).astype(BF16)
    for pr in pairs:
        f, b = pr["f"], pr["b"]
        av = jnp.dot(pr["amat"], jnp.concatenate([f["v_new"], b["v_new"]], axis=0), preferred_element_type=F32)
        f["o"] = f["ws"][chunk:] + av[:chunk]
        b["o"] = b["ws"][chunk:] + av[chunk:]
    for un in units:
        un["s_new"] = un["s"] * un["ge"] + lax.dot_general(un["kd"], un["v_new"], _DN_L, preferred_element_type=F32)


def _gdn_scan_kernel(qkvf_ref, qkvb_ref, auxf_ref, auxb_ref, gcrf_ref, gcrb_ref,
                     of_ref, ob_ref, s_ref, *, chunk):
    @pl.when(pl.program_id(1) == 0)
    def _():
        s_ref[...] = jnp.zeros_like(s_ref)

    nc = qkvf_ref.shape[1] // chunk
    dirs = ((qkvf_ref, auxf_ref, gcrf_ref, of_ref), (qkvb_ref, auxb_ref, gcrb_ref, ob_ref))

    per_iter = GDN_CHUNKS_PER_ITER if nc % GDN_CHUNKS_PER_ITER == 0 else 1

    def body(it, carry):
        groups = []
        for sub in range(per_iter):
            c = it * per_iter + sub
            by_dir = []
            for d, (qkv_ref, aux_ref, gcr_ref, o_ref) in enumerate(dirs):
                cc = c if d == 0 else nc - 1 - c
                r0 = pl.multiple_of(cc * chunk, chunk)
                aux = aux_ref[0, pl.ds(r0, chunk), :]
                gcr_all = gcr_ref[0, cc]
                units = []
                for h in range(GDN_HEADS):
                    si = d * GDN_HEADS + h
                    units.append(dict(
                        q=qkv_ref[0, pl.ds(r0, chunk), h * GDN_DK:(h + 1) * GDN_DK],
                        k=qkv_ref[0, pl.ds(r0, chunk), GDN_W + h * GDN_DK:GDN_W + (h + 1) * GDN_DK],
                        v=qkv_ref[0, pl.ds(r0, chunk), 2 * GDN_W + h * GDN_DV:2 * GDN_W + (h + 1) * GDN_DV],
                        aux=aux, gcr=gcr_all[si:si + 1, :], col=si, d=d, h=h, r0=r0, o_ref=o_ref))
                by_dir.append(units)
            groups.append([dict(f=uf, b=ub) for uf, ub in zip(*by_dir)])
        _gdn_intra_step([pr for pairs in groups for pr in pairs], chunk)
        state = [s_ref[si] for si in range(2 * GDN_HEADS)]
        for pairs in groups:
            units = [un for pr in pairs for un in (pr["f"], pr["b"])]
            for un in units:
                un["s"] = state[un["col"]]
            _gdn_state_step(pairs, chunk)
            for un in units:
                state[un["col"]] = un["s_new"]
                un["o_ref"][0, pl.ds(un["r0"], chunk), un["h"] * GDN_DV:(un["h"] + 1) * GDN_DV] = un["o"].astype(BF16)
        for si in range(2 * GDN_HEADS):
            s_ref[si] = state[si]
        return carry

    lax.fori_loop(0, nc // per_iter, body, 0)


def _gdn_scan(qkv, aux, gcr, chunk):
    b, t, _ = qkv.shape
    tb = min(1024, t)
    nb = t // tb
    nh2 = 2 * GDN_HEADS
    kern = functools.partial(_gdn_scan_kernel, chunk=chunk)
    fwd = lambda bi, i: (bi, i, 0)
    bwd = lambda bi, i: (bi, nb - 1 - i, 0)
    return pl.pallas_call(
        kern,
        grid=(b, nb),
        in_specs=[
            pl.BlockSpec((1, tb, GDN_CONV_CH), fwd),
            pl.BlockSpec((1, tb, GDN_CONV_CH), bwd),
            pl.BlockSpec((1, tb, LANES), fwd),
            pl.BlockSpec((1, tb, LANES), bwd),
            pl.BlockSpec((1, tb // chunk, nh2, 2 * chunk), lambda bi, i: (bi, i, 0, 0)),
            pl.BlockSpec((1, tb // chunk, nh2, 2 * chunk), lambda bi, i: (bi, nb - 1 - i, 0, 0)),
        ],
        out_specs=[pl.BlockSpec((1, tb, GDN_W), fwd), pl.BlockSpec((1, tb, GDN_W), bwd)],
        out_shape=[jax.ShapeDtypeStruct((b, t, GDN_W), BF16)] * 2,
        scratch_shapes=[pltpu.VMEM((nh2, GDN_DK, GDN_DV), F32)],
        compiler_params=_cparams(("parallel", "arbitrary")),
        name="gdn_scan",
    )(qkv, qkv, aux, aux, gcr, gcr)


ATT_QT = 128


ATT_BLK = 2048


ATT_TILES_PER_ITER = 4


def _attn_tiles(tiles):
    head_of_lane = lax.broadcasted_iota(jnp.int32, (ATT_QT, ATT_GW), 1) // ATT_HEAD_DIM
    in_head = [head_of_lane == h for h in range(ATT_HPG)]
    for tl in tiles:
        q = tl["q"]
        q4 = jnp.concatenate([jnp.where(m, q, jnp.zeros_like(q)) for m in in_head], axis=0)
        tl["s"] = lax.dot_general(q4, tl["kt"], _DN_T, preferred_element_type=F32)
    for tl in tiles:
        nk = tl["s"].shape[1]
        s = jnp.where(tl["valid"][None], tl["s"].reshape(ATT_HPG, ATT_QT, nk), NEG_BIG)
        mx = jnp.max(s, axis=-1, keepdims=True)
        p = jnp.exp(s - mx)
        tl["l"] = jnp.sum(p, axis=-1, keepdims=True)
        tl["mx"] = mx
        tl["p"] = p.reshape(ATT_HPG * ATT_QT, nk).astype(BF16)
    for tl in tiles:
        tl["pv"] = jnp.dot(tl["p"], tl["vt"], preferred_element_type=F32)
    for tl in tiles:
        o = jnp.zeros((ATT_QT, ATT_GW), F32)
        lse = jnp.zeros((ATT_QT, ATT_GW), F32)
        for h in range(ATT_HPG):
            rows = slice(h * ATT_QT, (h + 1) * ATT_QT)
            o = jnp.where(in_head[h], tl["pv"][rows] / tl["l"][h], o)
            lse = jnp.where(in_head[h], tl["mx"][h] + jnp.log(tl["l"][h]), lse)
        tl["o"], tl["lse"] = o, lse


def _attn_kernel(*refs, seq):
    n_g = len(ATT_GROUPS)
    ins = refs[:7 * n_g]
    y_ref = refs[7 * n_g]
    kv_scr = refs[7 * n_g + 1:7 * n_g + 1 + 2 * n_g]
    o_scr, l_scr = refs[7 * n_g + 1 + 2 * n_g:]
    i = pl.program_id(1)
    n_half = ATT_GW // LANES
    nk = ATT_QT + 2 * ATT_SPAN
    qi = lax.broadcasted_iota(jnp.int32, (ATT_QT, nk), 0)
    ki = lax.broadcasted_iota(jnp.int32, (ATT_QT, nk), 1)
    rel = ki - qi
    band = jnp.logical_and(rel >= 0, rel <= 2 * ATT_SPAN)
    for g, (_, r) in enumerate(ATT_GROUPS):
        q_ref, kp_ref, kc_ref, kn_ref, vp_ref, vc_ref, vn_ref = ins[7 * g:7 * g + 7]
        kf_ref, vf_ref = kv_scr[2 * g:2 * g + 2]
        rows = ATT_BLK // r
        m_total = seq // r
        for full_ref, parts in ((kf_ref, (kp_ref, kc_ref, kn_ref)), (vf_ref, (vp_ref, vc_ref, vn_ref))):
            full_ref[:, 0:ATT_SPAN, :] = parts[0][0]
            full_ref[:, ATT_SPAN:ATT_SPAN + rows, :] = parts[1][0]
            full_ref[:, ATT_SPAN + rows:, :] = parts[2][0]
        tiles = rows // ATT_QT

        assert (r * tiles) % ATT_TILES_PER_ITER == 0

        def tile_group(it, carry, r=r, g=g, rows=rows, tiles=tiles, m_total=m_total,
                       q_ref=q_ref, kf_ref=kf_ref, vf_ref=vf_ref):
            group = []
            for u in range(ATT_TILES_PER_ITER):
                idx = it * ATT_TILES_PER_ITER + u
                s = idx // tiles
                r0 = pl.multiple_of((idx % tiles) * ATT_QT, ATT_QT)
                mk = i * rows + r0 - ATT_SPAN + ki
                group.append(dict(
                    res=s, r0=r0, q=q_ref[0, s, pl.ds(r0, ATT_QT), :],
                    kt=kf_ref[s, pl.ds(r0, nk), :], vt=vf_ref[s, pl.ds(r0, nk), :],
                    valid=jnp.logical_and(band, jnp.logical_and(mk >= 0, mk < m_total))))
            _attn_tiles(group)
            for tl in group:
                for hf in range(n_half):
                    lanes = slice(hf * LANES, (hf + 1) * LANES)
                    if r == 1:
                        dst = pl.ds(tl["r0"], ATT_QT)
                    else:
                        dst = pl.ds(tl["r0"] * r + tl["res"], ATT_QT, stride=r)
                    o_scr[g * n_half + hf, dst, :] = tl["o"][:, lanes]
                    l_scr[g * n_half + hf, dst, :] = tl["lse"][:, lanes]
            return carry

        lax.fori_loop(0, r * tiles // ATT_TILES_PER_ITER, tile_group, 0)

    cb = 256

    def comb(c, carry):
        r0 = pl.multiple_of(c * cb, cb)
        halves = []
        for hf in range(n_half):
            ls = [l_scr[g * n_half + hf, pl.ds(r0, cb), :] for g in range(n_g)]
            lm = functools.reduce(jnp.maximum, ls)
            es = [jnp.exp(lv - lm) for lv in ls]
            num = sum(es[g] * o_scr[g * n_half + hf, pl.ds(r0, cb), :] for g in range(n_g))
            halves.append(num / sum(es))
        y_ref[pl.ds(r0, cb), :] = jnp.concatenate(halves, axis=1).astype(BF16)
        return carry

    lax.fori_loop(0, ATT_BLK // cb, comb, 0)


def _attention(att_qkv, b, seq):
    assert seq % ATT_BLK == 0
    nblk = seq // ATT_BLK
    in_specs, scratch, args = [], [], []
    for (_, r), arr in zip(ATT_GROUPS, att_qkv):
        rows = ATT_BLK // r
        assert rows % ATT_QT == 0
        hb = rows // ATT_SPAN
        last = seq // r // ATT_SPAN - 1

        def center(c, r=r, rows=rows):
            return pl.BlockSpec((1, r, rows, ATT_GW), lambda bi, i: (bi, 0, i, c))

        def prev(c, r=r, hb=hb):
            return pl.BlockSpec((1, r, ATT_SPAN, ATT_GW), lambda bi, i: (bi, 0, jnp.maximum(i * hb - 1, 0), c))

        def nxt(c, r=r, hb=hb, last=last):
            return pl.BlockSpec((1, r, ATT_SPAN, ATT_GW), lambda bi, i: (bi, 0, jnp.minimum((i + 1) * hb, last), c))

        in_specs += [center(0), prev(1), center(1), nxt(1), prev(2), center(2), nxt(2)]
        args += [arr] * 7
        scratch += [pltpu.VMEM((r, rows + 2 * ATT_SPAN, ATT_GW), BF16)] * 2
    n_g = len(ATT_GROUPS)
    scratch += [pltpu.VMEM((n_g * (ATT_GW // LANES), ATT_BLK, LANES), F32)] * 2
    kern = functools.partial(_attn_kernel, seq=seq)
    return pl.pallas_call(
        kern,
        grid=(b, nblk),
        in_specs=in_specs,
        out_specs=pl.BlockSpec((ATT_BLK, ATT_GW), lambda bi, i: (bi * nblk + i, 0)),
        out_shape=jax.ShapeDtypeStruct((b * seq, ATT_GW), BF16),
        scratch_shapes=scratch,
        compiler_params=_cparams(("parallel", "parallel")),
        name="attention",
    )(*args)


def _merge_kernel(of_ref, ob_ref, z_ref, ga_ref, gb_ref, yb_ref,
                  x_ref, gn_ref, wa_ref, wb_ref, wo_ref, nf_ref, wr_ref,
                  x1_ref, h2r_ref, afft_ref):
    o = of_ref[...].astype(F32) + ob_ref[...].astype(F32)
    z = z_ref[...].astype(F32)
    ya_parts = []
    for h in range(GDN_HEADS):
        sl = slice(h * GDN_DV, (h + 1) * GDN_DV)
        oh = o[:, sl]
        oh = oh * lax.rsqrt(jnp.mean(oh * oh, axis=-1, keepdims=True) + RMS_EPS) * gn_ref[...]
        zh = z[:, sl]
        ya_parts.append(oh * (zh * (1.0 / (1.0 + jnp.exp(-zh)))))
    y_a = jnp.concatenate(ya_parts, axis=1).astype(BF16)
    y_b = yb_ref[...]
    ga = 1.0 / (1.0 + jnp.exp(-ga_ref[...].astype(F32)))
    gb = 1.0 / (1.0 + jnp.exp(-gb_ref[...].astype(F32)))
    mixed = (ga * jnp.dot(y_a, wa_ref[...], preferred_element_type=F32)
             + gb * jnp.dot(y_b, wb_ref[...], preferred_element_type=F32))
    x1 = x_ref[...] + jnp.dot(mixed.astype(BF16), wo_ref[...], preferred_element_type=F32)
    x1_ref[...] = x1
    h2f = x1 * lax.rsqrt(jnp.mean(x1 * x1, axis=-1, keepdims=True) + RMS_EPS) * nf_ref[...]
    h2 = h2f.astype(BF16)
    tm = h2f.shape[0]
    for j in range(D_MODEL // LANES):
        h2r_ref[pl.ds(j, tm, stride=D_MODEL // LANES), :] = h2f[:, j * LANES:(j + 1) * LANES]
    logits = jnp.dot(h2, wr_ref[...], preferred_element_type=F32)
    lane = lax.broadcasted_iota(jnp.int32, logits.shape, 1)
    logits = jnp.where(lane < N_EXPERTS, logits, NEG_BIG)
    lmax = jnp.max(logits, axis=-1, keepdims=True)
    ex = jnp.exp(logits - lmax)
    aff = ex / jnp.sum(ex, axis=-1, keepdims=True)
    afft_ref[0] = aff.T[:N_EXPERTS, :]


def _merge(of, ob, proj, y_b, x2d, gdn_norm, wa, wb, wo, norm_ffn, w_router, seq):
    n = x2d.shape[0]
    tm = min(512, seq)
    tpb = seq // tm
    wr = jnp.zeros((D_MODEL, LANES), BF16).at[:, :N_EXPERTS].set(w_router.astype(BF16))
    row = lambda w: pl.BlockSpec((tm, w), lambda i: (i, 0))
    full = lambda a: pl.BlockSpec(a.shape, lambda i: (0,) * a.ndim)
    gn = gdn_norm.reshape(1, GDN_DV)
    nf = norm_ffn.reshape(1, D_MODEL)
    in_specs = [
        row(GDN_W), row(GDN_W),
        pl.BlockSpec((tm, GDN_W), lambda i: (i, C_Z // GDN_W)),
        pl.BlockSpec((tm, D_MODEL), lambda i: (i, C_GATE // D_MODEL)),
        pl.BlockSpec((tm, D_MODEL), lambda i: (i, C_GATE // D_MODEL + 1)),
        row(ATT_GW),
        row(D_MODEL), full(gn), full(wa), full(wb), full(wo), full(nf), full(wr),
    ]
    return pl.pallas_call(
        _merge_kernel,
        grid=(n // tm,),
        in_specs=in_specs,
        out_specs=[
            row(D_MODEL),
            pl.BlockSpec((tm * (D_MODEL // LANES), LANES), lambda i: (i, 0)),
            pl.BlockSpec((1, N_EXPERTS, tm), lambda i: (i // tpb, 0, i % tpb)),
        ],
        out_shape=[
            jax.ShapeDtypeStruct((n, D_MODEL), F32),
            jax.ShapeDtypeStruct((n * (D_MODEL // LANES), LANES), F32),
            jax.ShapeDtypeStruct((n // seq, N_EXPERTS, seq), F32),
        ],
        compiler_params=_cparams(("parallel",)),
        name="merge",
    )(of, ob, proj, proj, proj, y_b, x2d, gn, wa, wb, wo, nf, wr)


def _prefix_tables(n_exp, nch):
    rows = n_exp * nch
    tri = (np.arange(LANES)[:, None] <= np.arange(LANES)[None, :]).astype(np.float32)
    ri = np.arange(rows)
    same = (ri[:, None] // nch) == (ri[None, :] // nch)
    btri = np.logical_and(same, (ri[None, :] % nch) < (ri[:, None] % nch)).astype(np.float32)
    return jnp.asarray(tri, BF16), jnp.asarray(btri, BF16)


def _route_kernel(aff_ref, tri_ref, btri_ref, cin_ref, start_ref, *, cap):
    x = aff_ref[0]
    n_exp, nch, _ = x.shape
    rows = n_exp * nch
    bits = lax.bitcast_convert_type(x, jnp.int32)

    def body(it, ans):
        cand = jnp.bitwise_or(ans, lax.shift_left(jnp.int32(1), 30 - it))
        cnt = jnp.sum(jnp.where(bits >= cand, 1.0, 0.0), axis=(1, 2), keepdims=True)
        return jnp.where(cnt >= cap, cand, ans)

    thr = lax.fori_loop(0, 31, body, jnp.zeros((n_exp, 1, 1), jnp.int32))
    gt = jnp.where(bits > thr, 1.0, 0.0)
    eq = jnp.where(bits == thr, 1.0, 0.0)
    need = cap - jnp.sum(gt, axis=(1, 2), keepdims=True)
    need_rows = jnp.broadcast_to(need, x.shape).reshape(rows, LANES)
    gt2 = gt.reshape(rows, LANES)
    eq2 = eq.reshape(rows, LANES)

    def prefix(mask2):
        cin = jnp.dot(mask2.astype(BF16), tri_ref[...], preferred_element_type=F32)
        tot = jnp.broadcast_to(cin[:, LANES - 1:LANES], cin.shape).astype(BF16)
        off = jnp.dot(btri_ref[...], tot, preferred_element_type=F32)
        return cin, off

    cin_eq, off_eq = prefix(eq2)
    eq_before = off_eq + cin_eq - eq2
    sel = jnp.maximum(gt2, jnp.where(eq_before < need_rows, eq2, 0.0))
    cin, off = prefix(sel)
    cin_ref[0] = cin
    start_ref[0] = off


def _route(afft, cap):
    b, n_exp, t = afft.shape
    nch = t // LANES
    rows = n_exp * nch
    tri, btri = _prefix_tables(n_exp, nch)
    aff4 = afft.reshape(b, n_exp, nch, LANES)
    kern = functools.partial(_route_kernel, cap=cap)
    return pl.pallas_call(
        kern,
        grid=(b,),
        in_specs=[
            pl.BlockSpec((1, n_exp, nch, LANES), lambda i: (i, 0, 0, 0)),
            pl.BlockSpec((LANES, LANES), lambda i: (0, 0)),
            pl.BlockSpec((rows, rows), lambda i: (0, 0)),
        ],
        out_specs=[pl.BlockSpec((1, rows, LANES), lambda i: (i, 0, 0))] * 2,
        out_shape=[jax.ShapeDtypeStruct((b, rows, LANES), F32)] * 2,
        compiler_params=_cparams(("parallel",)),
        name="route",
    )(aff4, tri, btri)


def _pad_rows(a, rows):
    if a.shape[0] == rows:
        return a
    return jnp.concatenate([a, jnp.zeros((rows - a.shape[0], a.shape[1]), a.dtype)], axis=0)


def _compact_kernel(cin_ref, start_ref, aff_ref, gatec_ref, idxr_ref, *, cap):
    r = _pad_rows(cin_ref[0, 0], LANES)
    st = _pad_rows(start_ref[0, 0], LANES)
    en = st + r[:, LANES - 1:LANES]
    st_row = st.T[0:1, :]
    en_row = en.T[0:1, :]
    sig = lax.broadcasted_iota(jnp.int32, (cap, LANES), 0).astype(F32)
    lane = lax.broadcasted_iota(jnp.int32, (cap, LANES), 1).astype(F32)
    onehot = jnp.where(jnp.logical_and(st_row <= sig, sig < en_row), 1.0, 0.0)
    st_sel = jnp.sum(onehot * st_row, axis=1, keepdims=True)
    c_sel = jnp.sum(onehot * lane, axis=1, keepdims=True)
    oh16 = onehot.astype(BF16)
    r_sel = jnp.dot(oh16, r.astype(BF16), preferred_element_type=F32)
    before = sig[:, 0:1] - st_sel
    pos = jnp.sum(jnp.where(r_sel <= before, 1.0, 0.0), axis=1, keepdims=True)
    idx = c_sel * LANES + pos
    a = _pad_rows(aff_ref[0, 0], LANES)
    a1 = a.astype(BF16)
    rem = a - a1.astype(F32)
    a2 = rem.astype(BF16)
    a3 = (rem - a2.astype(F32)).astype(BF16)
    a_sel = (jnp.dot(oh16, a1, preferred_element_type=F32) + jnp.dot(oh16, a2, preferred_element_type=F32)
             + jnp.dot(oh16, a3, preferred_element_type=F32))
    gatec_ref[0] = jnp.sum(jnp.where(lane == pos, a_sel, 0.0), axis=1, keepdims=True)
    idx_b = jnp.broadcast_to(idx, (cap, LANES))
    for kb in range(cap // LANES):
        blk = idx_b[kb * LANES:(kb + 1) * LANES, :].T
        idxr_ref[0, :, kb * LANES:(kb + 1) * LANES] = blk[0:1, :].astype(jnp.int32)


def _compact(cin, start, afft, cap):
    b, n_exp, t = afft.shape
    nch = t // LANES
    cin4 = cin.reshape(b, n_exp, nch, LANES)
    start4 = start.reshape(b, n_exp, nch, LANES)
    aff4 = afft.reshape(b, n_exp, nch, LANES)
    in_spec = pl.BlockSpec((1, 1, nch, LANES), lambda bi, e: (bi, e, 0, 0))
    kern = functools.partial(_compact_kernel, cap=cap)
    return pl.pallas_call(
        kern,
        grid=(b, n_exp),
        in_specs=[in_spec, in_spec, in_spec],
        out_specs=[
            pl.BlockSpec((1, cap, 1), lambda bi, e: (bi * n_exp + e, 0, 0)),
            pl.BlockSpec((1, 1, cap), lambda bi, e: (bi * n_exp + e, 0, 0)),
        ],
        out_shape=[
            jax.ShapeDtypeStruct((b * n_exp, cap, 1), F32),
            jax.ShapeDtypeStruct((b * n_exp, 1, cap), jnp.int32),
        ],
        compiler_params=_cparams(("parallel", "parallel")),
        name="compact",
    )(cin4, start4, aff4)


MOE_TS = 256
MOE_GK = 8
TOKEN_ROWS = D_MODEL // LANES
GATHER_UNROLL = 32


def _ffn_kernel(idx_ref, h_ref, gate_ref, wg_ref, wu_ref, wd_ref, y_ref, xr_ref):
    tm = gate_ref.shape[1]

    def gather(i, carry):
        for u in range(GATHER_UNROLL):
            r = i * GATHER_UNROLL + u
            src = pl.multiple_of(idx_ref[0, 0, r] * TOKEN_ROWS, TOKEN_ROWS)
            dst = pl.multiple_of(r * TOKEN_ROWS, TOKEN_ROWS)
            xr_ref[pl.ds(dst, TOKEN_ROWS), :] = h_ref[pl.ds(src, TOKEN_ROWS), :]
        return carry

    lax.fori_loop(0, tm // GATHER_UNROLL, gather, 0)
    xe = jnp.concatenate([xr_ref[pl.ds(j, tm, stride=TOKEN_ROWS), :] for j in range(TOKEN_ROWS)],
                         axis=1).astype(BF16)
    a = jnp.dot(xe, wg_ref[0], preferred_element_type=F32)
    u = jnp.dot(xe, wu_ref[0], preferred_element_type=F32)
    act = (a * (1.0 / (1.0 + jnp.exp(-a))) * u).astype(BF16)
    y = jnp.dot(act, wd_ref[0], preferred_element_type=F32)
    y_ref[0] = (y * gate_ref[0]).astype(BF16)


def _ffn(h2r, idx_row, gate_col, w_gate, w_up, w_down, b, t, cap):
    n_exp = w_gate.shape[0]
    tm = min(512, cap)
    wspec = pl.BlockSpec((1, D_MODEL, EXPERT_FF), lambda bi, e, ct: (e, 0, 0))
    return pl.pallas_call(
        _ffn_kernel,
        grid=(b, n_exp, cap // tm),
        in_specs=[
            pl.BlockSpec((1, 1, tm), lambda bi, e, ct: (bi * n_exp + e, 0, ct), memory_space=pltpu.SMEM),
            pl.BlockSpec((t * TOKEN_ROWS, LANES), lambda bi, e, ct: (bi, 0), pipeline_mode=pl.Buffered(1)),
            pl.BlockSpec((1, tm, 1), lambda bi, e, ct: (bi * n_exp + e, ct, 0)),
            wspec, wspec,
            pl.BlockSpec((1, EXPERT_FF, D_MODEL), lambda bi, e, ct: (e, 0, 0)),
        ],
        out_specs=pl.BlockSpec((1, tm, D_MODEL), lambda bi, e, ct: (bi * n_exp + e, ct, 0)),
        out_shape=jax.ShapeDtypeStruct((b * n_exp, cap, D_MODEL), BF16),
        scratch_shapes=[pltpu.VMEM((tm * TOKEN_ROWS, LANES), F32)],
        compiler_params=_cparams(("arbitrary", "arbitrary", "arbitrary")),
        name="expert_ffn",
    )(idx_row, h2r, gate_col, w_gate, w_up, w_down)


def _combine_kernel(klo_ref, khi_ref, x1_ref, y_ref, idxr_ref, nfin_ref, o_ref, ystage_ref, istage_ref, acc_ref, *,
                    cap, final_norm):
    bi, ti = pl.program_id(0), pl.program_id(1)
    nts = pl.num_programs(1)
    ts2 = acc_ref.shape[0]
    n_exp = idxr_ref.shape[1]
    gk = MOE_GK * LANES

    @pl.when(jnp.logical_and(bi == 0, ti == 0))
    def _():
        ystage_ref[...] = jnp.zeros_like(ystage_ref)

    istage_ref[...] = jnp.full(istage_ref.shape, -1, jnp.int32)
    n = jnp.int32(0)
    for e in range(n_exp):
        tab = (bi * nts + ti) * n_exp + e

        def stage(k, n, e=e):
            s0 = pl.multiple_of(k * LANES, LANES)
            d0 = pl.multiple_of(n * LANES, LANES)
            ystage_ref[pl.ds(d0, LANES), :] = y_ref[0, pl.ds(e * cap + s0, LANES), :]
            istage_ref[:, pl.ds(d0, LANES)] = idxr_ref[0, e:e + 1, pl.ds(s0, LANES)]
            return n + 1

        n = lax.fori_loop(klo_ref[tab], khi_ref[tab] + 1, stage, n)
    tok = ti * ts2 + lax.broadcasted_iota(jnp.int32, (ts2, gk), 0)
    acc_ref[...] = jnp.zeros_like(acc_ref)

    def scatter(g, carry):
        k0 = pl.multiple_of(g * gk, gk)
        onehot = jnp.where(tok == istage_ref[:, pl.ds(k0, gk)], 1.0, 0.0).astype(BF16)
        acc_ref[...] += jnp.dot(onehot, ystage_ref[pl.ds(k0, gk), :], preferred_element_type=F32)
        return carry

    lax.fori_loop(0, (n + MOE_GK - 1) // MOE_GK, scatter, 0)
    out = x1_ref[...] + acc_ref[...]
    if final_norm:
        out = out * lax.rsqrt(jnp.mean(out * out, axis=-1, keepdims=True) + RMS_EPS) * nfin_ref[...]
    o_ref[...] = out


def _combine(x1, y, idx_row, start, norm_final, b, t, cap, final_norm):
    n_exp = N_EXPERTS
    ts2 = min(MOE_TS, t)
    nts = t // ts2
    nch = t // LANES
    st = start[:, :, 0].reshape(b, n_exp, nch)[:, :, ::ts2 // LANES].astype(jnp.int32)
    en = jnp.concatenate([st[:, :, 1:], jnp.full((b, n_exp, 1), cap, jnp.int32)], axis=2)
    klo = st // LANES
    khi = jnp.where(en > st, (en - 1) // LANES, klo - 1)
    klo_tab = jnp.transpose(klo, (0, 2, 1)).reshape(-1)
    khi_tab = jnp.transpose(khi, (0, 2, 1)).reshape(-1)
    y3 = y.reshape(b, n_exp * cap, D_MODEL)
    idxr3 = idx_row.reshape(b, n_exp, cap)
    max_chunks = n_exp * min(ts2 // LANES + 1, cap // LANES)
    stage_chunks = -(-max_chunks // MOE_GK) * MOE_GK
    kern = functools.partial(_combine_kernel, cap=cap, final_norm=final_norm)
    grid_spec = pltpu.PrefetchScalarGridSpec(
        num_scalar_prefetch=2,
        grid=(b, nts),
        in_specs=[
            pl.BlockSpec((ts2, D_MODEL), lambda bi, ti, lo, hi: (bi * nts + ti, 0)),
            pl.BlockSpec((1, n_exp * cap, D_MODEL), lambda bi, ti, lo, hi: (bi, 0, 0), pipeline_mode=pl.Buffered(1)),
            pl.BlockSpec((1, n_exp, cap), lambda bi, ti, lo, hi: (bi, 0, 0)),
            pl.BlockSpec((1, D_MODEL), lambda bi, ti, lo, hi: (0, 0)),
        ],
        out_specs=pl.BlockSpec((ts2, D_MODEL), lambda bi, ti, lo, hi: (bi * nts + ti, 0)),
        scratch_shapes=[pltpu.VMEM((stage_chunks * LANES, D_MODEL), BF16),
                        pltpu.VMEM((1, stage_chunks * LANES), jnp.int32),
                        pltpu.VMEM((ts2, D_MODEL), F32)],
    )
    return pl.pallas_call(
        kern,
        grid_spec=grid_spec,
        out_shape=jax.ShapeDtypeStruct((b * t, D_MODEL), F32),
        compiler_params=_cparams(("arbitrary", "arbitrary")),
        name="moe_combine",
    )(klo_tab, khi_tab, x1, y3, idxr3, norm_final.reshape(1, D_MODEL))


GDN_CHUNK = 64


def kernel(x, norm_mix, w_in, conv_w, a_log, dt_bias, gdn_norm, w_branch_a, w_branch_b, w_out, norm_ffn, w_router, w_expert_gate, w_expert_up, w_expert_down, norm_final):
    b, t, d = x.shape
    depth = w_in.shape[0]
    cap = EC_CAPACITY_FACTOR * t // N_EXPERTS
    rope_cos, rope_sin = _rope_tables(t)
    xc = x.reshape(b * t, d)
    for layer in range(depth):
        w_all, w_bd, w_att = _proj_weights(w_in[layer])
        nw = norm_mix[layer].reshape(1, d)
        proj, bd, *att_qkv = _att_proj(xc, nw, w_all, w_bd, w_att, rope_cos, rope_sin, b, t)
        proj3 = proj.reshape(b, t, PROJ_W)
        qkv, aux, gcr = _gdn_prep(proj3, bd.reshape(b, t, LANES), conv_w[layer], a_log[layer], dt_bias[layer],
                                  GDN_CHUNK)
        o_f, o_b = _gdn_scan(qkv, aux, gcr, GDN_CHUNK)
        y_b = _attention(att_qkv, b, t)
        x1, h2r, afft = _merge(o_f.reshape(b * t, GDN_W), o_b.reshape(b * t, GDN_W), proj, y_b, xc,
                              gdn_norm[layer], w_branch_a[layer].astype(BF16), w_branch_b[layer].astype(BF16),
                              w_out[layer].astype(BF16), norm_ffn[layer], w_router[layer], t)
        cin, start = _route(afft, cap)
        gate_col, idx_row = _compact(cin, start, afft, cap)
        y = _ffn(h2r, idx_row, gate_col, w_expert_gate[layer].astype(BF16),
                 w_expert_up[layer].astype(BF16), w_expert_down[layer].astype(BF16), b, t, cap)
        xc = _combine(x1, y, idx_row, start, norm_final, b, t, cap, final_norm=(layer == depth - 1))
    return xc.reshape(b, t, d)
```
